```python
import math
import jax
import jax.numpy as jnp
from jax import lax
import numpy as np

D_MODEL = 1024
BATCH = 8
SEQ = 2048
DEPTH = 2

HEAD_DIM = 64
MIX_WIDTH = D_MODEL
GROUP_WIDTH = MIX_WIDTH // 2
EPS = 1e-6
NEG = -1e30
ROPE_THETA = 10000.0

HY_CHANNELS = GROUP_WIDTH
HY_ORDER = 2
HY_IN = (HY_ORDER + 1) * HY_CHANNELS
HY_SHORT = 3
HY_EMB = 33
HY_BANDS = (HY_EMB - 1) // 2
HY_FILTER_HIDDEN = 64
HY_FAST_DECAY = 0.3
HY_SLOW_DECAY = 1.5
HY_DECAY_TARGET = 1e-2
HY_MAX_DECAY = math.log(HY_DECAY_TARGET) / HY_FAST_DECAY
HY_MIN_DECAY = math.log(HY_DECAY_TARGET) / HY_SLOW_DECAY
HY_MOD_SHIFT = 0.05

RET_HEADS = GROUP_WIDTH // HEAD_DIM
RET_CHUNK = 128
EVEN_IN = HY_IN + 4 * GROUP_WIDTH

GRID_W = 64
NA_HEADS = GROUP_WIDTH // HEAD_DIM
NA_ROWS = 8
NA_COLS = 16

DIL_HEADS = GROUP_WIDTH // HEAD_DIM
DIL_BRANCHES = ((128, 1), (512, 4), (2048, 16))
ODD_IN = 3 * NA_HEADS * HEAD_DIM + 3 * DIL_HEADS * HEAD_DIM

PEER_HEADS = 8
PEER_KEYS = 128
PEER_EXPERTS = PEER_KEYS * PEER_KEYS
PEER_DK = 128
PEER_TOPK = 16
PEER_BLOCK = 128

N_EVEN = (DEPTH + 1) // 2
N_ODD = DEPTH // 2

kernel_name = 'hybrid_hyena_retnet_natten_dilated_peer'

F32 = jnp.float32


def _rms_norm(x, gain):
    xf = x.astype(F32)
    y = xf * lax.rsqrt(jnp.mean(xf * xf, axis=-1, keepdims=True) + EPS)
    return (y * gain.astype(F32)).astype(x.dtype)


def _head_rms(t, gain):
    t = t.astype(F32)
    return t * lax.rsqrt(jnp.mean(t * t, axis=-1, keepdims=True) + EPS) * gain.astype(F32)


def _rope(t):
    S, hd = t.shape[1], t.shape[-1]
    inv = ROPE_THETA ** (-jnp.arange(0, hd, 2, dtype=F32) / hd)
    ang = jnp.arange(S, dtype=F32)[:, None] * inv[None, :]
    cos = jnp.cos(ang)[None, :, None, :]
    sin = jnp.sin(ang)[None, :, None, :]
    t1, t2 = jnp.split(t.astype(F32), 2, axis=-1)
    return jnp.concatenate([t1 * cos - t2 * sin, t2 * cos + t1 * sin], axis=-1)


def _hyena_filters(L, fw1, fb1, fw2, fb2, fw3, fb3, fw_out, freq):
    t_norm = jnp.linspace(0.0, 1.0, L, dtype=F32)[:, None]
    w = 2.0 * math.pi * jnp.arange(L, dtype=F32)[:, None] / L
    f = jnp.linspace(1e-4, HY_BANDS - 1, HY_BANDS, dtype=F32)[None, :]
    emb = jnp.concatenate([t_norm, jnp.cos(f * w), -jnp.sin(f * w)], axis=-1)
    fr = freq.astype(F32)
    hdn = jnp.sin(fr * (emb @ fw1.astype(F32) + fb1.astype(F32)))
    hdn = jnp.sin(fr * (hdn @ fw2.astype(F32) + fb2.astype(F32)))
    hdn = jnp.sin(fr * (hdn @ fw3.astype(F32) + fb3.astype(F32)))
    filt = (hdn @ fw_out.astype(F32)).reshape(L, HY_ORDER, 2, HY_CHANNELS)
    deltas = jnp.abs(jnp.linspace(HY_MIN_DECAY, HY_MAX_DECAY, HY_CHANNELS, dtype=F32))
    window = jnp.exp(-t_norm * deltas[None, :]) + HY_MOD_SHIFT
    filt = filt * window[:, None, None, :]
    fwd = filt[:, :, 0]
    bwd = filt[:, :, 1]
    circ = jnp.concatenate([fwd, jnp.zeros_like(fwd[:1]), bwd[:0:-1]], axis=0)
    return jnp.fft.rfft(circ, axis=0)


def _fft_long_conv(z, filt_f, skip):
    L = z.shape[1]
    zf = jnp.fft.rfft(z, n=2 * L, axis=1)
    y = jnp.fft.irfft(zf * filt_f[None], n=2 * L, axis=1)[:, :L]
    return y + z * skip.astype(F32)


def _hyena(u, conv_w, conv_b, fw1, fb1, fw2, fb2, fw3, fb3, fw_out, freq, skip):
    L = u.shape[1]
    up = jnp.pad(u, ((0, 0), (1, 1), (0, 0)))
    cw = conv_w.astype(F32)
    uc = up[:, :-2] * cw[0] + up[:, 1:-1] * cw[1] + up[:, 2:] * cw[2] + conv_b.astype(F32)
    v, x1, x2 = jnp.split(uc, 3, axis=-1)
    filt_f = _hyena_filters(L, fw1, fb1, fw2, fb2, fw3, fb3, fw_out, freq)
    z = x1 * _fft_long_conv(v, filt_f[:, 0], skip[0])
    return x2 * _fft_long_conv(z, filt_f[:, 1], skip[1])


def _retention_scan(q, k, v, log_g, strict):
    B, H, S, dk = q.shape
    dv = v.shape[-1]
    C = RET_CHUNK
    n = S // C
    qc = q.reshape(B, H, n, C, dk)
    kc = k.reshape(B, H, n, C, dk)
    vc = v.reshape(B, H, n, C, dv)
    pos = jnp.arange(C, dtype=F32)
    lg = log_g[:, None]
    diff = pos[:, None] - pos[None, :]
    keep = (diff > 0) if strict else (diff >= 0)
    dmat = jnp.where(keep[None], jnp.exp(lg[:, :, None] * jnp.maximum(diff, 0.0)[None]), 0.0)
    scores = jnp.einsum('bhncd,bhnkd->bhnck', qc, kc) * dmat[None, :, None]
    y_intra = jnp.einsum('bhnck,bhnke->bhnce', scores, vc)
    k_dec = jnp.exp(lg * (C - 1 - pos))
    q_dec = jnp.exp(lg * (pos + 1))
    kv = jnp.einsum('bhncd,bhnce->nbhde', kc * k_dec[None, :, None, :, None], vc)
    chunk_decay = jnp.exp(lg[:, 0] * C)[None, :, None, None]

    def step(state, kv_i):
        return state * chunk_decay + kv_i, state

    _, prev = lax.scan(step, jnp.zeros((B, H, dk, dv), F32), kv)
    y_cross = jnp.einsum('bhncd,nbhde->bhnce', qc * q_dec[None, :, None, :, None], prev)
    return (y_intra + y_cross).reshape(B, H, S, dv)


def _retention(r_in, log_decay_param, gn_gain):
    B, S, _ = r_in.shape
    rq, rk, rv, rg = jnp.split(r_in, 4, axis=-1)
    shp = (B, S, RET_HEADS, HEAD_DIM)
    q = _rope(rq.reshape(shp)).transpose(0, 2, 1, 3)
    k = (_rope(rk.reshape(shp)) * HEAD_DIM ** -0.5).transpose(0, 2, 1, 3)
    v = rv.reshape(shp).transpose(0, 2, 1, 3)
    log_g = -jnp.exp(log_decay_param.astype(F32))
    y_fwd = _retention_scan(q, k, v, log_g[0], False)
    rev = lambda t: t[:, :, ::-1]
    y_bwd = rev(_retention_scan(rev(q), rev(k), rev(v), log_g[1], True))
    y = (y_fwd + y_bwd).transpose(0, 2, 1, 3)
    y = y * lax.rsqrt(jnp.mean(y * y, axis=-1, keepdims=True) + EPS)
    y = y.reshape(B, S, GROUP_WIDTH) * gn_gain.astype(F32)
    return jax.nn.silu(rg) * y


def _neighbourhood_attention(q, k, v, rpb):
    B, S, H, hd = q.shape
    rows = S // GRID_W
    kh = min(NA_ROWS, rows)
    r = jnp.arange(rows)
    row_idx = jnp.clip(r - kh // 2, 0, rows - kh)[:, None] + jnp.arange(kh)[None, :]
    c = jnp.arange(GRID_W)
    c0 = jnp.clip(c - NA_COLS // 2, 0, GRID_W - NA_COLS)
    col_valid = (c[None, :] >= c0[:, None]) & (c[None, :] < c0[:, None] + NA_COLS)
    col_off = jnp.clip(c[None, :] - c[:, None] + NA_COLS - 1, 0, 2 * NA_COLS - 2)
    row_off = row_idx - r[:, None] + NA_ROWS - 1
    qg = q.reshape(B, rows, GRID_W, H, hd)
    kb = k.reshape(B, rows, GRID_W, H, hd)[:, row_idx]
    vb = v.reshape(B, rows, GRID_W, H, hd)[:, row_idx]
    s = jnp.einsum('brqhd,brkchd->bhrqkc', qg, kb)
    bias = rpb.astype(F32)[:, row_off[:, None, :, None], col_off[None, :, None, :]]
    s = jnp.where(col_valid[:, None, :], s + bias[None], NEG)
    p = jax.nn.softmax(s, axis=(-2, -1))
    o = jnp.einsum('bhrqkc,brkchd->brqhd', p, vb)
    return o.reshape(B, S, H, hd)


def _band_attention(q, k, v, radius):
    N, L, hd = q.shape
    blk = radius
    nb = -(-L // blk)
    pad = nb * blk - L
    qb = jnp.pad(q, ((0, 0), (0, pad), (0, 0))).reshape(N, nb, blk, hd)
    kp = jnp.pad(k, ((0, 0), (blk, blk + pad), (0, 0))).reshape(N, nb + 2, blk, hd)
    vp = jnp.pad(v, ((0, 0), (blk, blk + pad), (0, 0))).reshape(N, nb + 2, blk, hd)
    kb = jnp.concatenate([kp[:, :-2], kp[:, 1:-1], kp[:, 2:]], axis=2)
    vb = jnp.concatenate([vp[:, :-2], vp[:, 1:-1], vp[:, 2:]], axis=2)
    qi = (jnp.arange(nb)[:, None] * blk + jnp.arange(blk)[None, :])[:, :, None]
    kj = (jnp.arange(nb)[:, None] * blk - blk + jnp.arange(3 * blk)[None, :])[:, None, :]
    valid = (jnp.abs(kj - qi) <= radius) & (kj >= 0) & (kj < L)
    s = jnp.where(valid[None], jnp.einsum('nbqd,nbkd->nbqk', qb, kb), NEG)
    m = jnp.max(s, axis=-1, keepdims=True)
    p = jnp.exp(s - m)
    den = jnp.sum(p, axis=-1, keepdims=True)
    o = jnp.einsum('nbqk,nbkd->nbqd', p, vb) / den
    lse = (m + jnp.log(den))[..., 0]
    return o.reshape(N, nb * blk, hd)[:, :L], lse.reshape(N, nb * blk)[:, :L]


def _dilated_attention(q, k, v):
    B, S, H, hd = q.shape
    outs, lses = [], []
    for window, dil in DIL_BRANCHES:
        Ls = S // dil

        def to_sub(t, Ls=Ls, dil=dil):
            return t.reshape(B, Ls, dil, H, hd).transpose(0, 2, 3, 1, 4).reshape(B * dil * H, Ls, hd)

        o, lse = _band_attention(to_sub(q), to_sub(k), to_sub(v), window // (2 * dil))
        outs.append(o.reshape(B, dil, H, Ls, hd).transpose(0, 3, 1, 2, 4).reshape(B, S, H, hd))
        lses.append(lse.reshape(B, dil, H, Ls).transpose(0, 3, 1, 2).reshape(B, S, H))
    wts = jax.nn.softmax(jnp.stack(lses), axis=0)
    return jnp.sum(wts[..., None] * jnp.stack(outs), axis=0)


def _peer(xn, w_q, sub_keys, u, v):
    B, S, D = xn.shape
    T = B * S
    xt = xn.reshape(T, D)
    q = (xt @ w_q).astype(F32).reshape(T, PEER_HEADS, 2, PEER_DK // 2)
    sc = jnp.einsum('thpd,hpkd->thpk', q, sub_keys.astype(F32))
    s_top, i_top = lax.top_k(sc, PEER_TOPK)
    cand_s = (s_top[:, :, 0, :, None] + s_top[:, :, 1, None, :]).reshape(T, PEER_HEADS, PEER_TOPK * PEER_TOPK)
    cand_i = (i_top[:, :, 0, :, None] * PEER_KEYS + i_top[:, :, 1, None, :]).reshape(T, PEER_HEADS, PEER_TOPK * PEER_TOPK)
    g_s, pick = lax.top_k(cand_s, PEER_TOPK)
    experts = jnp.take_along_axis(cand_i, pick, axis=-1).reshape(T, PEER_HEADS * PEER_TOPK)
    gates = jax.nn.softmax(g_s, axis=-1).reshape(T, PEER_HEADS * PEER_TOPK)
    nblk = T // PEER_BLOCK

    def block(args):
        xb, eb, gb = args
        ub = jnp.take(u, eb, axis=0)
        act = jax.nn.gelu(jnp.einsum('td,tkd->tk', xb, ub).astype(F32)) * gb
        vb = jnp.take(v, eb, axis=0)
        return jnp.einsum('tk,tkd->td', act, vb.astype(F32))

    out = lax.map(block, (xt.reshape(nblk, PEER_BLOCK, D),
                          experts.reshape(nblk, PEER_BLOCK, -1),
                          gates.reshape(nblk, PEER_BLOCK, -1)))
    return out.reshape(B, S, D)


def setup_inputs(seed: int = 0) -> dict:
    key = jax.random.key(seed)
    ks = iter(jax.random.split(key, 32))

    def nrm(shape, std):
        return std * jax.random.normal(next(ks), shape, F32)

    Hf = HY_FILTER_HIDDEN
    ret_init = jnp.log(-jnp.log1p(-jnp.power(2.0, -5.0 - jnp.arange(RET_HEADS, dtype=F32))))
    return {
        'x': nrm((BATCH, SEQ, D_MODEL), 1.0),
        'mix_norm_gain': 1.0 + nrm((DEPTH, D_MODEL), 0.02),
        'ffn_norm_gain': 1.0 + nrm((DEPTH, D_MODEL), 0.02),
        'ev_w_in': nrm((N_EVEN, D_MODEL, EVEN_IN), D_MODEL ** -0.5),
        'ev_w_out': nrm((N_EVEN, MIX_WIDTH, D_MODEL), MIX_WIDTH ** -0.5),
        'hy_conv_w': nrm((N_EVEN, HY_SHORT, HY_IN), HY_SHORT ** -0.5),
        'hy_conv_b': nrm((N_EVEN, HY_IN), 0.02),
        'hy_fw1': nrm((N_EVEN, HY_EMB, Hf), HY_EMB ** -0.5),
        'hy_fb1': nrm((N_EVEN, Hf), 0.1),
        'hy_fw2': nrm((N_EVEN, Hf, Hf), Hf ** -0.5),
        'hy_fb2': nrm((N_EVEN, Hf), 0.1),
        'hy_fw3': nrm((N_EVEN, Hf, Hf), Hf ** -0.5),
        'hy_fb3': nrm((N_EVEN, Hf), 0.1),
        'hy_fw_out': nrm((N_EVEN, Hf, HY_ORDER * 2 * HY_CHANNELS), 0.01),
        'hy_freq': 1.0 + nrm((N_EVEN, Hf), 0.05),
        'hy_skip': nrm((N_EVEN, HY_ORDER, HY_CHANNELS), 1.0),
        'ret_log_decay': ret_init[None, None, :] + nrm((N_EVEN, 2, RET_HEADS), 0.05),
        'ret_gn_gain': 1.0 + nrm((N_EVEN, GROUP_WIDTH), 0.02),
        'od_w_in': nrm((N_ODD, D_MODEL, ODD_IN), D_MODEL ** -0.5),
        'od_w_out': nrm((N_ODD, MIX_WIDTH, D_MODEL), MIX_WIDTH ** -0.5),
        'na_qk_gain': 1.0 + nrm((N_ODD, 2, HEAD_DIM), 0.02),
        'na_rpb': nrm((N_ODD, NA_HEADS, 2 * NA_ROWS - 1, 2 * NA_COLS - 1), 0.05),
        'dil_qk_gain': 1.0 + nrm((N_ODD, 2, HEAD_DIM), 0.02),
        'peer_w_q': nrm((DEPTH, D_MODEL, PEER_HEADS * PEER_DK), D_MODEL ** -0.5),
        'peer_sub_keys': nrm((DEPTH, PEER_HEADS, 2, PEER_KEYS, PEER_DK // 2), (PEER_DK // 2) ** -0.5),
        'peer_u': nrm((DEPTH, PEER_EXPERTS, D_MODEL), D_MODEL ** -0.5),
        'peer_v': nrm((DEPTH, PEER_EXPERTS, D_MODEL), PEER_HEADS ** -0.5),
    }


def reference(x, mix_norm_gain, ffn_norm_gain, ev_w_in, ev_w_out, hy_conv_w, hy_conv_b,
              hy_fw1, hy_fb1, hy_fw2, hy_fb2, hy_fw3, hy_fb3, hy_fw_out, hy_freq, hy_skip,
              ret_log_decay, ret_gn_gain, od_w_in, od_w_out, na_qk_gain, na_rpb, dil_qk_gain,
              peer_w_q, peer_sub_keys, peer_u, peer_v):
    B, S, _ = x.shape
    scale = HEAD_DIM ** -0.5
    for layer in range(DEPTH):
        h = _rms_norm(x, mix_norm_gain[layer])
        if layer % 2 == 0:
            e = layer // 2
            proj = (h @ ev_w_in[e]).astype(F32)
            a_out = _hyena(proj[..., :HY_IN], hy_conv_w[e], hy_conv_b[e], hy_fw1[e], hy_fb1[e],
                           hy_fw2[e], hy_fb2[e], hy_fw3[e], hy_fb3[e], hy_fw_out[e], hy_freq[e], hy_skip[e])
            b_out = _retention(proj[..., HY_IN:], ret_log_decay[e], ret_gn_gain[e])
            mixed = jnp.concatenate([a_out, b_out], axis=-1).astype(x.dtype) @ ev_w_out[e]
        else:
            o = layer // 2
            proj = (h @ od_w_in[o]).astype(F32)
            na_q, na_k, na_v, dl_q, dl_k, dl_v = jnp.split(proj, 6, axis=-1)
            shp = (B, S, NA_HEADS, HEAD_DIM)
            nq = _head_rms(na_q.reshape(shp), na_qk_gain[o, 0]) * scale
            nk = _head_rms(na_k.reshape(shp), na_qk_gain[o, 1])
            c_out = _neighbourhood_attention(nq, nk, na_v.reshape(shp), na_rpb[o])
            dshp = (B, S, DIL_HEADS, HEAD_DIM)
            dq = _rope(_head_rms(dl_q.reshape(dshp), dil_qk_gain[o, 0])) * scale
            dk = _rope(_head_rms(dl_k.reshape(dshp), dil_qk_gain[o, 1]))
            d_out = _dilated_attention(dq, dk, dl_v.reshape(dshp))
            mixed = jnp.concatenate([c_out.reshape(B, S, -1), d_out.reshape(B, S, -1)],
                                    axis=-1).astype(x.dtype) @ od_w_out[o]
        x = x + mixed.astype(x.dtype)
        x = x + _peer(_rms_norm(x, ffn_norm_gain[layer]), peer_w_q[layer], peer_sub_keys[layer],
                      peer_u[layer], peer_v[layer]).astype(x.dtype)
    return x
```

```python
import functools
import math

import jax
import jax.numpy as jnp
from jax import lax
from jax.experimental import pallas as pl
from jax.experimental.pallas import tpu as pltpu

F32 = jnp.float32

D_MODEL = 1024
HEAD_DIM = 64
GROUP_WIDTH = D_MODEL // 2
EPS = 1e-6
NEG = -1e30
ROPE_THETA = 10000.0

HY_CHANNELS = GROUP_WIDTH
HY_ORDER = 2
HY_IN = (HY_ORDER + 1) * HY_CHANNELS
HY_EMB = 33
HY_BANDS = (HY_EMB - 1) // 2
HY_FAST_DECAY = 0.3
HY_SLOW_DECAY = 1.5
HY_DECAY_TARGET = 1e-2
HY_MAX_DECAY = math.log(HY_DECAY_TARGET) / HY_FAST_DECAY
HY_MIN_DECAY = math.log(HY_DECAY_TARGET) / HY_SLOW_DECAY
HY_MOD_SHIFT = 0.05

RET_HEADS = GROUP_WIDTH // HEAD_DIM
RET_CHUNK = 128

GRID_W = 64
NA_HEADS = GROUP_WIDTH // HEAD_DIM
NA_ROWS = 8
NA_COLS = 16

DIL_HEADS = GROUP_WIDTH // HEAD_DIM
DIL_BRANCHES = ((128, 1), (512, 4), (2048, 16))

PEER_HEADS = 8
PEER_KEYS = 128
PEER_DK = 128
PEER_TOPK = 16
PEER_SEL = PEER_HEADS * PEER_TOPK

SUBLANES = 8
LANES = 128
ROW_TILE = SUBLANES * LANES
assert D_MODEL == ROW_TILE
PEER_TOK = 8


def _peer_eval_kernel(idx_cur, idx_nxt, x_ref, g_ref, uv_hbm, o_ref, buf, sem):
    i = pl.program_id(0)
    n = pl.num_programs(0)
    slot = lax.rem(i, 2)

    def issue(idx_ref, slot_):
        def body(g, c):
            for k in range(PEER_SEL):
                e = idx_ref[g, k]
                pltpu.make_async_copy(
                    uv_hbm.at[e], buf.at[slot_, g, pl.ds(k * 2 * SUBLANES, 2 * SUBLANES)],
                    sem.at[slot_]).start()
            return c
        lax.fori_loop(0, PEER_TOK, body, 0)

    @pl.when(i == 0)
    def _():
        issue(idx_cur, 0)

    @pl.when(i + 1 < n)
    def _():
        issue(idx_nxt, 1 - slot)

    pltpu.make_async_copy(buf.at[slot], buf.at[slot], sem.at[slot]).wait()

    gates_t = g_ref[...].T
    stride = 2 * SUBLANES
    for g in range(PEER_TOK):
        xg = x_ref[g]
        rows_g = buf.at[slot, g]
        acc = rows_g[pl.ds(0, PEER_SEL, stride), :] * xg[0:1, :]
        for s in range(1, SUBLANES):
            acc = acc + rows_g[pl.ds(s, PEER_SEL, stride), :] * xg[s:s + 1, :]
        h = jnp.sum(acc, axis=-1, keepdims=True)
        a = jax.nn.gelu(h) * gates_t[:, g:g + 1]
        ab = jnp.broadcast_to(a, (PEER_SEL, LANES))
        rows = [jnp.sum(ab * rows_g[pl.ds(SUBLANES + s, PEER_SEL, stride), :],
                        axis=0, keepdims=True) for s in range(SUBLANES)]
        o_ref[g] = jnp.concatenate(rows, axis=0)


def _peer_eval(xt, experts, gates, uv):
    T = xt.shape[0]
    n = T // PEER_TOK
    x3 = xt.reshape(T, SUBLANES, LANES)
    out = pl.pallas_call(
        _peer_eval_kernel,
        grid=(n,),
        in_specs=[
            pl.BlockSpec((PEER_TOK, PEER_SEL), lambda i: (i, 0), memory_space=pltpu.SMEM),
            pl.BlockSpec((PEER_TOK, PEER_SEL), lambda i: (jnp.minimum(i + 1, n - 1), 0),
                         memory_space=pltpu.SMEM),
            pl.BlockSpec((PEER_TOK, SUBLANES, LANES), lambda i: (i, 0, 0)),
            pl.BlockSpec((PEER_TOK, PEER_SEL), lambda i: (i, 0)),
            pl.BlockSpec(memory_space=pl.ANY),
        ],
        out_specs=pl.BlockSpec((PEER_TOK, SUBLANES, LANES), lambda i: (i, 0, 0)),
        out_shape=jax.ShapeDtypeStruct((T, SUBLANES, LANES), F32),
        scratch_shapes=[
            pltpu.VMEM((2, PEER_TOK, PEER_SEL * 2 * SUBLANES, LANES), F32),
            pltpu.SemaphoreType.DMA((2,)),
        ],
        compiler_params=pltpu.CompilerParams(
            dimension_semantics=("arbitrary",),
            vmem_limit_bytes=40 * 1024 * 1024),
        name="peer_eval",
    )(experts, experts, x3, gates, uv)
    return out.reshape(T, D_MODEL)


def _rms_norm(x, gain):
    xf = x.astype(F32)
    y = xf * lax.rsqrt(jnp.mean(xf * xf, axis=-1, keepdims=True) + EPS)
    return (y * gain.astype(F32)).astype(x.dtype)


def _head_rms(t, gain):
    t = t.astype(F32)
    return t * lax.rsqrt(jnp.mean(t * t, axis=-1, keepdims=True) + EPS) * gain.astype(F32)


def _rope(t):
    S, hd = t.shape[1], t.shape[-1]
    inv = ROPE_THETA ** (-jnp.arange(0, hd, 2, dtype=F32) / hd)
    ang = jnp.arange(S, dtype=F32)[:, None] * inv[None, :]
    cos = jnp.cos(ang)[None, :, None, :]
    sin = jnp.sin(ang)[None, :, None, :]
    t1, t2 = jnp.split(t.astype(F32), 2, axis=-1)
    return jnp.concatenate([t1 * cos - t2 * sin, t2 * cos + t1 * sin], axis=-1)


def _hyena_filters(L, fw1, fb1, fw2, fb2, fw3, fb3, fw_out, freq):
    t_norm = jnp.linspace(0.0, 1.0, L, dtype=F32)[:, None]
    w = 2.0 * math.pi * jnp.arange(L, dtype=F32)[:, None] / L
    f = jnp.linspace(1e-4, HY_BANDS - 1, HY_BANDS, dtype=F32)[None, :]
    emb = jnp.concatenate([t_norm, jnp.cos(f * w), -jnp.sin(f * w)], axis=-1)
    fr = freq.astype(F32)
    hdn = jnp.sin(fr * (emb @ fw1.astype(F32) + fb1.astype(F32)))
    hdn = jnp.sin(fr * (hdn @ fw2.astype(F32) + fb2.astype(F32)))
    hdn = jnp.sin(fr * (hdn @ fw3.astype(F32) + fb3.astype(F32)))
    filt = (hdn @ fw_out.astype(F32)).reshape(L, HY_ORDER, 2, HY_CHANNELS)
    deltas = jnp.abs(jnp.linspace(HY_MIN_DECAY, HY_MAX_DECAY, HY_CHANNELS, dtype=F32))
    window = jnp.exp(-t_norm * deltas[None, :]) + HY_MOD_SHIFT
    filt = filt * window[:, None, None, :]
    fwd = filt[:, :, 0]
    bwd = filt[:, :, 1]
    circ = jnp.concatenate([fwd, jnp.zeros_like(fwd[:1]), bwd[:0:-1]], axis=0)
    return jnp.fft.rfft(circ, axis=0)


def _fft_long_conv(z, filt_f, skip):
    L = z.shape[1]
    zf = jnp.fft.rfft(z, n=2 * L, axis=1)
    y = jnp.fft.irfft(zf * filt_f[None], n=2 * L, axis=1)[:, :L]
    return y + z * skip.astype(F32)


def _hyena(u, conv_w, conv_b, fw1, fb1, fw2, fb2, fw3, fb3, fw_out, freq, skip):
    L = u.shape[1]
    up = jnp.pad(u, ((0, 0), (1, 1), (0, 0)))
    cw = conv_w.astype(F32)
    uc = up[:, :-2] * cw[0] + up[:, 1:-1] * cw[1] + up[:, 2:] * cw[2] + conv_b.astype(F32)
    v, x1, x2 = jnp.split(uc, 3, axis=-1)
    filt_f = _hyena_filters(L, fw1, fb1, fw2, fb2, fw3, fb3, fw_out, freq)
    z = x1 * _fft_long_conv(v, filt_f[:, 0], skip[0])
    return x2 * _fft_long_conv(z, filt_f[:, 1], skip[1])


def _retention_scan(q, k, v, log_g, strict):
    B, H, S, dk = q.shape
    dv = v.shape[-1]
    C = RET_CHUNK
    n = S // C
    qc = q.reshape(B, H, n, C, dk)
    kc = k.reshape(B, H, n, C, dk)
    vc = v.reshape(B, H, n, C, dv)
    pos = jnp.arange(C, dtype=F32)
    lg = log_g[:, None]
    diff = pos[:, None] - pos[None, :]
    keep = (diff > 0) if strict else (diff >= 0)
    dmat = jnp.where(keep[None], jnp.exp(lg[:, :, None] * jnp.maximum(diff, 0.0)[None]), 0.0)
    scores = jnp.einsum('bhncd,bhnkd->bhnck', qc, kc) * dmat[None, :, None]
    y_intra = jnp.einsum('bhnck,bhnke->bhnce', scores, vc)
    k_dec = jnp.exp(lg * (C - 1 - pos))
    q_dec = jnp.exp(lg * (pos + 1))
    kv = jnp.einsum('bhncd,bhnce->nbhde', kc * k_dec[None, :, None, :, None], vc)
    chunk_decay = jnp.exp(lg[:, 0] * C)[None, :, None, None]

    def step(state, kv_i):
        return state * chunk_decay + kv_i, state

    _, prev = lax.scan(step, jnp.zeros((B, H, dk, dv), F32), kv)
    y_cross = jnp.einsum('bhncd,nbhde->bhnce', qc * q_dec[None, :, None, :, None], prev)
    return (y_intra + y_cross).reshape(B, H, S, dv)


def _retention(r_in, log_decay_param, gn_gain):
    B, S, _ = r_in.shape
    rq, rk, rv, rg = jnp.split(r_in, 4, axis=-1)
    shp = (B, S, RET_HEADS, HEAD_DIM)
    q = _rope(rq.reshape(shp)).transpose(0, 2, 1, 3)
    k = (_rope(rk.reshape(shp)) * HEAD_DIM ** -0.5).transpose(0, 2, 1, 3)
    v = rv.reshape(shp).transpose(0, 2, 1, 3)
    log_g = -jnp.exp(log_decay_param.astype(F32))
    y_fwd = _retention_scan(q, k, v, log_g[0], False)
    rev = lambda t: t[:, :, ::-1]
    y_bwd = rev(_retention_scan(rev(q), rev(k), rev(v), log_g[1], True))
    y = (y_fwd + y_bwd).transpose(0, 2, 1, 3)
    y = y * lax.rsqrt(jnp.mean(y * y, axis=-1, keepdims=True) + EPS)
    y = y.reshape(B, S, GROUP_WIDTH) * gn_gain.astype(F32)
    return jax.nn.silu(rg) * y


def _neighbourhood_attention(q, k, v, rpb):
    B, S, H, hd = q.shape
    rows = S // GRID_W
    kh = min(NA_ROWS, rows)
    r = jnp.arange(rows)
    row_idx = jnp.clip(r - kh // 2, 0, rows - kh)[:, None] + jnp.arange(kh)[None, :]
    c = jnp.arange(GRID_W)
    c0 = jnp.clip(c - NA_COLS // 2, 0, GRID_W - NA_COLS)
    col_valid = (c[None, :] >= c0[:, None]) & (c[None, :] < c0[:, None] + NA_COLS)
    col_off = jnp.clip(c[None, :] - c[:, None] + NA_COLS - 1, 0, 2 * NA_COLS - 2)
    row_off = row_idx - r[:, None] + NA_ROWS - 1
    qg = q.reshape(B, rows, GRID_W, H, hd)
    kb = k.reshape(B, rows, GRID_W, H, hd)[:, row_idx]
    vb = v.reshape(B, rows, GRID_W, H, hd)[:, row_idx]
    s = jnp.einsum('brqhd,brkchd->bhrqkc', qg, kb)
    bias = rpb.astype(F32)[:, row_off[:, None, :, None], col_off[None, :, None, :]]
    s = jnp.where(col_valid[:, None, :], s + bias[None], NEG)
    p = jax.nn.softmax(s, axis=(-2, -1))
    o = jnp.einsum('bhrqkc,brkchd->brqhd', p, vb)
    return o.reshape(B, S, H, hd)


def _band_attention(q, k, v, radius):
    N, L, hd = q.shape
    blk = radius
    nb = -(-L // blk)
    pad = nb * blk - L
    qb = jnp.pad(q, ((0, 0), (0, pad), (0, 0))).reshape(N, nb, blk, hd)
    kp = jnp.pad(k, ((0, 0), (blk, blk + pad), (0, 0))).reshape(N, nb + 2, blk, hd)
    vp = jnp.pad(v, ((0, 0), (blk, blk + pad), (0, 0))).reshape(N, nb + 2, blk, hd)
    kb = jnp.concatenate([kp[:, :-2], kp[:, 1:-1], kp[:, 2:]], axis=2)
    vb = jnp.concatenate([vp[:, :-2], vp[:, 1:-1], vp[:, 2:]], axis=2)
    qi = (jnp.arange(nb)[:, None] * blk + jnp.arange(blk)[None, :])[:, :, None]
    kj = (jnp.arange(nb)[:, None] * blk - blk + jnp.arange(3 * blk)[None, :])[:, None, :]
    valid = (jnp.abs(kj - qi) <= radius) & (kj >= 0) & (kj < L)
    s = jnp.where(valid[None], jnp.einsum('nbqd,nbkd->nbqk', qb, kb), NEG)
    m = jnp.max(s, axis=-1, keepdims=True)
    p = jnp.exp(s - m)
    den = jnp.sum(p, axis=-1, keepdims=True)
    o = jnp.einsum('nbqk,nbkd->nbqd', p, vb) / den
    lse = (m + jnp.log(den))[..., 0]
    return o.reshape(N, nb * blk, hd)[:, :L], lse.reshape(N, nb * blk)[:, :L]


def _dilated_attention(q, k, v):
    B, S, H, hd = q.shape
    outs, lses = [], []
    for window, dil in DIL_BRANCHES:
        Ls = S // dil

        def to_sub(t, Ls=Ls, dil=dil):
            return t.reshape(B, Ls, dil, H, hd).transpose(0, 2, 3, 1, 4).reshape(B * dil * H, Ls, hd)

        o, lse = _band_attention(to_sub(q), to_sub(k), to_sub(v), window // (2 * dil))
        outs.append(o.reshape(B, dil, H, Ls, hd).transpose(0, 3, 1, 2, 4).reshape(B, S, H, hd))
        lses.append(lse.reshape(B, dil, H, Ls).transpose(0, 3, 1, 2).reshape(B, S, H))
    wts = jax.nn.softmax(jnp.stack(lses), axis=0)
    return jnp.sum(wts[..., None] * jnp.stack(outs), axis=0)


def _peer(xn, w_q, sub_keys, u, v):
    B, S, D = xn.shape
    T = B * S
    E = u.shape[0]
    xt = xn.reshape(T, D)
    q = (xt @ w_q).astype(F32).reshape(T, PEER_HEADS, 2, PEER_DK // 2)
    sc = jnp.einsum('thpd,hpkd->thpk', q, sub_keys.astype(F32))
    s_top, i_top = lax.top_k(sc, PEER_TOPK)
    cand_s = (s_top[:, :, 0, :, None] + s_top[:, :, 1, None, :]).reshape(T, PEER_HEADS, PEER_TOPK * PEER_TOPK)
    cand_i = (i_top[:, :, 0, :, None] * PEER_KEYS + i_top[:, :, 1, None, :]).reshape(T, PEER_HEADS, PEER_TOPK * PEER_TOPK)
    g_s, pick = lax.top_k(cand_s, PEER_TOPK)
    experts = jnp.take_along_axis(cand_i, pick, axis=-1).reshape(T, PEER_SEL)
    gates = jax.nn.softmax(g_s, axis=-1).reshape(T, PEER_SEL)
    uv = jnp.concatenate([u.astype(F32).reshape(E, SUBLANES, LANES),
                          v.astype(F32).reshape(E, SUBLANES, LANES)], axis=1)
    out = _peer_eval(xt.astype(F32), experts.astype(jnp.int32), gates, uv)
    return out.reshape(B, S, D)


def kernel(x, mix_norm_gain, ffn_norm_gain, ev_w_in, ev_w_out, hy_conv_w, hy_conv_b, hy_fw1, hy_fb1, hy_fw2, hy_fb2, hy_fw3, hy_fb3, hy_fw_out, hy_freq, hy_skip, ret_log_decay, ret_gn_gain, od_w_in, od_w_out, na_qk_gain, na_rpb, dil_qk_gain, peer_w_q, peer_sub_keys, peer_u, peer_v):
    B, S, _ = x.shape
    depth = mix_norm_gain.shape[0]
    scale = HEAD_DIM ** -0.5
    for layer in range(depth):
        h = _rms_norm(x, mix_norm_gain[layer])
        if layer % 2 == 0:
            e = layer // 2
            proj = (h @ ev_w_in[e]).astype(F32)
            a_out = _hyena(proj[..., :HY_IN], hy_conv_w[e], hy_conv_b[e], hy_fw1[e], hy_fb1[e],
                           hy_fw2[e], hy_fb2[e], hy_fw3[e], hy_fb3[e], hy_fw_out[e], hy_freq[e], hy_skip[e])
            b_out = _retention(proj[..., HY_IN:], ret_log_decay[e], ret_gn_gain[e])
            mixed = jnp.concatenate([a_out, b_out], axis=-1).astype(x.dtype) @ ev_w_out[e]
        else:
            o = layer // 2
            proj = (h @ od_w_in[o]).astype(F32)
            na_q, na_k, na_v, dl_q, dl_k, dl_v = jnp.split(proj, 6, axis=-1)
            shp = (B, S, NA_HEADS, HEAD_DIM)
            nq = _head_rms(na_q.reshape(shp), na_qk_gain[o, 0]) * scale
            nk = _head_rms(na_k.reshape(shp), na_qk_gain[o, 1])
            c_out = _neighbourhood_attention(nq, nk, na_v.reshape(shp), na_rpb[o])
            dshp = (B, S, DIL_HEADS, HEAD_DIM)
            dq = _rope(_head_rms(dl_q.reshape(dshp), dil_qk_gain[o, 0])) * scale
            dk = _rope(_head_rms(dl_k.reshape(dshp), dil_qk_gain[o, 1]))
            d_out = _dilated_attention(dq, dk, dl_v.reshape(dshp))
            mixed = jnp.concatenate([c_out.reshape(B, S, -1), d_out.reshape(B, S, -1)],
                                    axis=-1).astype(x.dtype) @ od_w_out[o]
        x = x + mixed.astype(x.dtype)
        x = x + _peer(_rms_norm(x, ffn_norm_gain[layer]), peer_w_q[layer], peer_sub_keys[layer],
                      peer_u[layer], peer_v[layer]).astype(x.dtype)
    return x
```

```python
import functools
import math

import numpy as np
import jax
import jax.numpy as jnp
from jax import lax
from jax.experimental import pallas as pl
from jax.experimental.pallas import tpu as pltpu

F32 = jnp.float32
BF16 = jnp.bfloat16

D_MODEL = 1024
HEAD_DIM = 64
GROUP_WIDTH = D_MODEL // 2
EPS = 1e-6
NEG = -1e30
NEG_INF = float("-inf")
ROPE_THETA = 10000.0

HY_CHANNELS = GROUP_WIDTH
HY_ORDER = 2
HY_IN = (HY_ORDER + 1) * HY_CHANNELS
HY_EMB = 33
HY_BANDS = (HY_EMB - 1) // 2
HY_FAST_DECAY = 0.3
HY_SLOW_DECAY = 1.5
HY_DECAY_TARGET = 1e-2
HY_MAX_DECAY = math.log(HY_DECAY_TARGET) / HY_FAST_DECAY
HY_MIN_DECAY = math.log(HY_DECAY_TARGET) / HY_SLOW_DECAY
HY_MOD_SHIFT = 0.05

RET_HEADS = GROUP_WIDTH // HEAD_DIM
RET_CHUNK = 128

GRID_W = 64
NA_HEADS = GROUP_WIDTH // HEAD_DIM
NA_ROWS = 8
NA_COLS = 16

DIL_HEADS = GROUP_WIDTH // HEAD_DIM
DIL_BRANCHES = ((128, 1), (512, 4), (2048, 16))

PEER_HEADS = 8
PEER_KEYS = 128
PEER_DK = 128
PEER_TOPK = 16
PEER_SEL = PEER_HEADS * PEER_TOPK

SUBLANES = 8
LANES = 128
assert D_MODEL == SUBLANES * LANES
PEER_TOK = 8
ROUTE_TOK = 256
DMA_THREADS = 2


def _stair_tables():
    a_idx = [0] * 16 + [a for a in range(1, 8) for _ in range(8)] + list(range(8, 16))
    b_idx = list(range(16)) + [b for _ in range(1, 8) for b in range(8)] + [0] * 8
    a_idx = np.array(a_idx)
    b_idx = np.array(b_idx)
    valid = (a_idx + 1) * (b_idx + 1) <= PEER_TOPK
    return a_idx, b_idx, valid


_ST_A, _ST_B, _ST_VALID = _stair_tables()
N_CAND = len(_ST_A)


def _top16_rows(sc, iota):
    vals, idxs = [], []
    big = float(sc.shape[0])
    for r in range(PEER_TOPK):
        m = jnp.max(sc, axis=0, keepdims=True)
        idx = jnp.min(jnp.where(sc == m, iota, big), axis=0, keepdims=True)
        vals.append(m)
        idxs.append(idx)
        if r + 1 < PEER_TOPK:
            sc = jnp.where(iota == idx, NEG_INF, sc)
    return jnp.concatenate(vals, axis=0), jnp.concatenate(idxs, axis=0)


def _route_kernel(xn_ref, wqT_ref, keys_ref, flat_ref, e_ref, g_ref, qT_ref, eT_ref, gT_ref):
    Tb = xn_ref.shape[0]
    xb = xn_ref[...].astype(BF16)
    qT_ref[...] = lax.dot_general(wqT_ref[...], xb, (((1,), (1,)), ((), ())), preferred_element_type=F32)
    iota = lax.broadcasted_iota(jnp.int32, (PEER_KEYS, Tb), 0).astype(F32)
    flat = jnp.broadcast_to(flat_ref[...], (N_CAND, Tb))
    for h in range(PEER_HEADS):
        s, ix = [], []
        for p in range(2):
            r0 = (h * 2 + p) * (PEER_DK // 2)
            sc = jnp.dot(keys_ref[h * 2 + p], qT_ref[pl.ds(r0, PEER_DK // 2), :].astype(BF16),
                         preferred_element_type=F32)
            sp, ip = _top16_rows(sc, iota)
            s.append(sp)
            ix.append(ip)
        s0, s1 = s
        i0, i1 = ix[0] * float(PEER_KEYS), ix[1]
        parts_s = [s0[0:1] + s1]
        parts_e = [i0[0:1] + i1]
        for a in range(1, 8):
            parts_s.append(s0[a:a + 1] + s1[0:8])
            parts_e.append(i0[a:a + 1] + i1[0:8])
        parts_s.append(s0[8:16] + s1[0:1])
        parts_e.append(i0[8:16] + i1[0:1])
        cand = jnp.where(flat < 256.0, jnp.concatenate(parts_s, axis=0), NEG_INF)
        eid = jnp.concatenate(parts_e, axis=0)
        gs, es = [], []
        for r in range(PEER_TOPK):
            m = jnp.max(cand, axis=0, keepdims=True)
            fsel = jnp.min(jnp.where(cand == m, flat, float(1 << 20)), axis=0, keepdims=True)
            hit = flat == fsel
            es.append(jnp.max(jnp.where(hit, eid, -1.0), axis=0, keepdims=True))
            gs.append(m)
            if r + 1 < PEER_TOPK:
                cand = jnp.where(hit, NEG_INF, cand)
        g = jnp.concatenate(gs, axis=0)
        ex = jnp.exp(g - g[0:1])
        eT_ref[pl.ds(h * PEER_TOPK, PEER_TOPK), :] = jnp.concatenate(es, axis=0).astype(jnp.int32)
        gT_ref[pl.ds(h * PEER_TOPK, PEER_TOPK), :] = ex / jnp.sum(ex, axis=0, keepdims=True)
    e_ref[...] = eT_ref[...].T
    g_ref[...] = gT_ref[...].T


def _peer_route(xn, w_q, sub_keys):
    T = xn.shape[0]
    Tb = ROUTE_TOK
    wqT = w_q.T.astype(BF16)
    keys = sub_keys.reshape(PEER_HEADS * 2, PEER_KEYS, PEER_DK // 2).astype(BF16)
    flat = np.where(_ST_VALID, _ST_A * 16 + _ST_B, 1 << 16).astype(np.float32).reshape(N_CAND, 1)
    return pl.pallas_call(
        _route_kernel, grid=(T // Tb,),
        in_specs=[pl.BlockSpec((Tb, D_MODEL), lambda i: (i, 0)),
                  pl.BlockSpec((D_MODEL, D_MODEL), lambda i: (0, 0)),
                  pl.BlockSpec((PEER_HEADS * 2, PEER_KEYS, PEER_DK // 2), lambda i: (0, 0, 0)),
                  pl.BlockSpec((N_CAND, 1), lambda i: (0, 0))],
        out_specs=[pl.BlockSpec((Tb, PEER_SEL), lambda i: (i, 0)),
                   pl.BlockSpec((Tb, PEER_SEL), lambda i: (i, 0))],
        out_shape=[jax.ShapeDtypeStruct((T, PEER_SEL), jnp.int32),
                   jax.ShapeDtypeStruct((T, PEER_SEL), F32)],
        scratch_shapes=[pltpu.VMEM((D_MODEL, Tb), F32), pltpu.VMEM((PEER_SEL, Tb), jnp.int32),
                        pltpu.VMEM((PEER_SEL, Tb), F32)],
        compiler_params=pltpu.CompilerParams(dimension_semantics=("arbitrary",),
                                             vmem_limit_bytes=32 * 1024 * 1024),
        name="peer_route")(xn, wqT, keys, jnp.asarray(flat))


def _peer_eval_kernel(idx_cur, idx_nxt, x_ref, g_ref, uv_hbm, o_ref, buf, sem):
    i = pl.program_id(0)
    n = pl.num_programs(0)
    G = PEER_TOK

    def issue_tok(idx_ref, row, slot, g):
        for k in range(PEER_SEL):
            e = idx_ref[row, k]
            pltpu.make_async_copy(uv_hbm.at[e], buf.at[slot, g, k // SUBLANES, :, k % SUBLANES, :],
                                  sem.at[slot]).start(priority=k % DMA_THREADS)

    def wait_slot(slot):
        pltpu.make_async_copy(buf.at[slot], buf.at[slot], sem.at[slot]).wait()

    def compute_tok(slot, g, row, gates_t):
        xg = x_ref[row]
        acc = buf[slot, g, :, 0].reshape(PEER_SEL, LANES) * xg[0:1, :]
        for c in range(1, SUBLANES):
            acc = acc + buf[slot, g, :, c].reshape(PEER_SEL, LANES) * xg[c:c + 1, :]
        h = jnp.sum(acc, axis=-1, keepdims=True)
        a = jax.nn.gelu(h) * gates_t[:, row:row + 1]
        ab = jnp.broadcast_to(a, (PEER_SEL, LANES))
        rows = [jnp.sum(ab * buf[slot, g, :, SUBLANES + c].reshape(PEER_SEL, LANES), axis=0, keepdims=True)
                for c in range(SUBLANES)]
        o_ref[row] = jnp.concatenate(rows, axis=0)

    @pl.when(i == 0)
    def _():
        for g in range(G):
            issue_tok(idx_cur, g, 0, g)

    gates_t = g_ref[...].T
    wait_slot(0)
    for g in range(G):
        issue_tok(idx_cur, G + g, 1, g)
        compute_tok(0, g, g, gates_t)
    wait_slot(1)
    for g in range(G):
        issue_tok(idx_nxt, g, 0, g)
        compute_tok(1, g, G + g, gates_t)

    @pl.when(i == n - 1)
    def _():
        wait_slot(0)


def _peer_eval(xt, experts, gates, uv):
    T = xt.shape[0]
    G = PEER_TOK
    n = T // (2 * G)
    x3 = xt.reshape(T, SUBLANES, LANES)
    out = pl.pallas_call(
        _peer_eval_kernel,
        grid=(n,),
        in_specs=[
            pl.BlockSpec((2 * G, PEER_SEL), lambda i: (i, 0), memory_space=pltpu.SMEM),
            pl.BlockSpec((2 * G, PEER_SEL), lambda i: (jnp.minimum(i + 1, n - 1), 0),
                         memory_space=pltpu.SMEM),
            pl.BlockSpec((2 * G, SUBLANES, LANES), lambda i: (i, 0, 0)),
            pl.BlockSpec((2 * G, PEER_SEL), lambda i: (i, 0)),
            pl.BlockSpec(memory_space=pl.ANY),
        ],
        out_specs=pl.BlockSpec((2 * G, SUBLANES, LANES), lambda i: (i, 0, 0)),
        out_shape=jax.ShapeDtypeStruct((T, SUBLANES, LANES), F32),
        scratch_shapes=[
            pltpu.VMEM((2, G, 2 * SUBLANES, 2 * SUBLANES, SUBLANES, LANES), F32),
            pltpu.SemaphoreType.DMA((2,)),
        ],
        compiler_params=pltpu.CompilerParams(
            dimension_semantics=("arbitrary",),
            vmem_limit_bytes=40 * 1024 * 1024),
        name="peer_eval",
    )(experts, experts, x3, gates, uv)
    return out.reshape(T, D_MODEL)


def _pair_sums(x):
    half = (lax.broadcasted_iota(jnp.int32, (LANES, LANES), 0) // HEAD_DIM ==
            lax.broadcasted_iota(jnp.int32, (LANES, LANES), 1) // HEAD_DIM).astype(BF16)
    hi = x.astype(BF16)
    lo = (x - hi.astype(F32)).astype(BF16)
    return (jnp.dot(hi, half, preferred_element_type=F32) +
            jnp.dot(lo, half, preferred_element_type=F32))


def _head_rms2(t, gain2):
    ms = _pair_sums(t * t) * (1.0 / HEAD_DIM)
    return t * lax.rsqrt(ms + EPS) * gain2


def _na_kernel(q_ref, k_ref, v_ref, gq_ref, gk_ref, bias_ref, o_ref, qn_ref, kn_ref, *, rows):
    scale = HEAD_DIM ** -0.5
    qn_ref[...] = (_head_rms2(q_ref[0], gq_ref[...]) * scale).astype(BF16)
    kn_ref[...] = _head_rms2(k_ref[0], gk_ref[...]).astype(BF16)
    lane = lax.broadcasted_iota(jnp.int32, (GRID_W, LANES), 1)
    first = lane < HEAD_DIM
    half_rows = NA_ROWS // 2

    def body(r, c):
        r0 = jnp.clip(r - half_rows, 0, rows - NA_ROWS)
        pat = jnp.where(r < half_rows, r,
                        jnp.where(r > rows - half_rows, r - (rows - NA_ROWS), half_rows))
        q2 = qn_ref[pl.ds(pl.multiple_of(r * GRID_W, GRID_W), GRID_W), :]
        kstart = pl.multiple_of(r0 * GRID_W, GRID_W)
        k2 = kn_ref[pl.ds(kstart, NA_ROWS * GRID_W), :]
        v2 = v_ref[0, pl.ds(kstart, NA_ROWS * GRID_W), :].astype(BF16)
        outs = []
        for hh in range(2):
            qm = jnp.where(first if hh == 0 else ~first, q2, jnp.zeros_like(q2))
            s = lax.dot_general(qm, k2, (((1,), (1,)), ((), ())), preferred_element_type=F32)
            s = s + bias_ref[hh, pat]
            m = jnp.max(s, axis=-1, keepdims=True)
            e = jnp.exp(s - m)
            l = jnp.sum(e, axis=-1, keepdims=True)
            outs.append(jnp.dot(e.astype(BF16), v2, preferred_element_type=F32) / l)
        o_ref[0, pl.ds(pl.multiple_of(r * GRID_W, GRID_W), GRID_W), :] = jnp.where(first, outs[0], outs[1])
        return c

    lax.fori_loop(0, rows, body, 0)


def _na_bias_table(rpb, rows):
    kh = NA_ROWS
    c = np.arange(GRID_W)
    c0 = np.clip(c - NA_COLS // 2, 0, GRID_W - NA_COLS)
    col_valid = (c[None, :] >= c0[:, None]) & (c[None, :] < c0[:, None] + NA_COLS)
    col_off = np.clip(c[None, :] - c[:, None] + NA_COLS - 1, 0, 2 * NA_COLS - 2)
    pat_rows = list(range(kh // 2)) + [kh // 2] + list(range(rows - kh // 2 + 1, rows))
    tabs = []
    for r in pat_rows:
        r0 = min(max(r - kh // 2, 0), rows - kh)
        ro = r0 + np.arange(kh) - r + NA_ROWS - 1
        b = rpb.astype(F32)[:, ro][:, :, col_off]
        b = jnp.where(col_valid[None, None], b, NEG)
        tabs.append(b.transpose(0, 2, 1, 3).reshape(rpb.shape[0], GRID_W, kh * GRID_W))
    return jnp.stack(tabs, axis=1)


def _neighbourhood_attention(qkv_slabs, gq, gk, rpb, B, S):
    H = rpb.shape[0]
    HP = H // 2
    rows = S // GRID_W
    assert rows >= NA_ROWS
    n_pat = NA_ROWS
    bias = _na_bias_table(rpb, rows).reshape(HP, 2, n_pat, GRID_W, NA_ROWS * GRID_W)
    g2q = jnp.tile(gq.astype(F32), 2).reshape(1, LANES)
    g2k = jnp.tile(gk.astype(F32), 2).reshape(1, LANES)
    x = qkv_slabs.reshape(3 * HP, B, S, LANES)
    return pl.pallas_call(
        functools.partial(_na_kernel, rows=rows), grid=(HP, B),
        in_specs=[pl.BlockSpec((None, 1, S, LANES), lambda h, b: (h, b, 0, 0)),
                  pl.BlockSpec((None, 1, S, LANES), lambda h, b: (HP + h, b, 0, 0)),
                  pl.BlockSpec((None, 1, S, LANES), lambda h, b: (2 * HP + h, b, 0, 0)),
                  pl.BlockSpec((1, LANES), lambda h, b: (0, 0)),
                  pl.BlockSpec((1, LANES), lambda h, b: (0, 0)),
                  pl.BlockSpec((None, 2, n_pat, GRID_W, NA_ROWS * GRID_W), lambda h, b: (h, 0, 0, 0, 0))],
        out_specs=pl.BlockSpec((None, 1, S, LANES), lambda h, b: (h, b, 0, 0)),
        out_shape=jax.ShapeDtypeStruct((HP, B, S, LANES), F32),
        scratch_shapes=[pltpu.VMEM((S, LANES), BF16), pltpu.VMEM((S, LANES), BF16)],
        compiler_params=pltpu.CompilerParams(dimension_semantics=("arbitrary", "arbitrary"),
                                             vmem_limit_bytes=32 * 1024 * 1024),
        name="nbr_attention")(x, x, x, g2q, g2k, bias)


def _rms_norm(x, gain):
    xf = x.astype(F32)
    y = xf * lax.rsqrt(jnp.mean(xf * xf, axis=-1, keepdims=True) + EPS)
    return (y * gain.astype(F32)).astype(x.dtype)


def _head_rms(t, gain):
    t = t.astype(F32)
    return t * lax.rsqrt(jnp.mean(t * t, axis=-1, keepdims=True) + EPS) * gain.astype(F32)


def _rope(t):
    S, hd = t.shape[1], t.shape[-1]
    inv = ROPE_THETA ** (-jnp.arange(0, hd, 2, dtype=F32) / hd)
    ang = jnp.arange(S, dtype=F32)[:, None] * inv[None, :]
    cos = jnp.cos(ang)[None, :, None, :]
    sin = jnp.sin(ang)[None, :, None, :]
    t1, t2 = jnp.split(t.astype(F32), 2, axis=-1)
    return jnp.concatenate([t1 * cos - t2 * sin, t2 * cos + t1 * sin], axis=-1)


def _hyena_filters(L, fw1, fb1, fw2, fb2, fw3, fb3, fw_out, freq):
    t_norm = jnp.linspace(0.0, 1.0, L, dtype=F32)[:, None]
    w = 2.0 * math.pi * jnp.arange(L, dtype=F32)[:, None] / L
    f = jnp.linspace(1e-4, HY_BANDS - 1, HY_BANDS, dtype=F32)[None, :]
    emb = jnp.concatenate([t_norm, jnp.cos(f * w), -jnp.sin(f * w)], axis=-1)
    fr = freq.astype(F32)
    hdn = jnp.sin(fr * (emb @ fw1.astype(F32) + fb1.astype(F32)))
    hdn = jnp.sin(fr * (hdn @ fw2.astype(F32) + fb2.astype(F32)))
    hdn = jnp.sin(fr * (hdn @ fw3.astype(F32) + fb3.astype(F32)))
    filt = (hdn @ fw_out.astype(F32)).reshape(L, HY_ORDER, 2, HY_CHANNELS)
    deltas = jnp.abs(jnp.linspace(HY_MIN_DECAY, HY_MAX_DECAY, HY_CHANNELS, dtype=F32))
    window = jnp.exp(-t_norm * deltas[None, :]) + HY_MOD_SHIFT
    filt = filt * window[:, None, None, :]
    fwd = filt[:, :, 0]
    bwd = filt[:, :, 1]
    circ = jnp.concatenate([fwd, jnp.zeros_like(fwd[:1]), bwd[:0:-1]], axis=0)
    return jnp.fft.rfft(circ, axis=0)


def _fft_long_conv(z, filt_f, skip):
    L = z.shape[1]
    zf = jnp.fft.rfft(z, n=2 * L, axis=1)
    y = jnp.fft.irfft(zf * filt_f[None], n=2 * L, axis=1)[:, :L]
    return y + z * skip.astype(F32)


def _hyena(u, conv_w, conv_b, fw1, fb1, fw2, fb2, fw3, fb3, fw_out, freq, skip):
    L = u.shape[1]
    up = jnp.pad(u, ((0, 0), (1, 1), (0, 0)))
    cw = conv_w.astype(F32)
    uc = up[:, :-2] * cw[0] + up[:, 1:-1] * cw[1] + up[:, 2:] * cw[2] + conv_b.astype(F32)
    v, x1, x2 = jnp.split(uc, 3, axis=-1)
    filt_f = _hyena_filters(L, fw1, fb1, fw2, fb2, fw3, fb3, fw_out, freq)
    z = x1 * _fft_long_conv(v, filt_f[:, 0], skip[0])
    return x2 * _fft_long_conv(z, filt_f[:, 1], skip[1])


def _retention_scan(q, k, v, log_g, strict):
    B, H, S, dk = q.shape
    dv = v.shape[-1]
    C = RET_CHUNK
    n = S // C
    qc = q.reshape(B, H, n, C, dk)
    kc = k.reshape(B, H, n, C, dk)
    vc = v.reshape(B, H, n, C, dv)
    pos = jnp.arange(C, dtype=F32)
    lg = log_g[:, None]
    diff = pos[:, None] - pos[None, :]
    keep = (diff > 0) if strict else (diff >= 0)
    dmat = jnp.where(keep[None], jnp.exp(lg[:, :, None] * jnp.maximum(diff, 0.0)[None]), 0.0)
    scores = jnp.einsum('bhncd,bhnkd->bhnck', qc, kc) * dmat[None, :, None]
    y_intra = jnp.einsum('bhnck,bhnke->bhnce', scores, vc)
    k_dec = jnp.exp(lg * (C - 1 - pos))
    q_dec = jnp.exp(lg * (pos + 1))
    kv = jnp.einsum('bhncd,bhnce->nbhde', kc * k_dec[None, :, None, :, None], vc)
    chunk_decay = jnp.exp(lg[:, 0] * C)[None, :, None, None]

    def step(state, kv_i):
        return state * chunk_decay + kv_i, state

    _, prev = lax.scan(step, jnp.zeros((B, H, dk, dv), F32), kv)
    y_cross = jnp.einsum('bhncd,nbhde->bhnce', qc * q_dec[None, :, None, :, None], prev)
    return (y_intra + y_cross).reshape(B, H, S, dv)


def _retention(r_in, log_decay_param, gn_gain):
    B, S, _ = r_in.shape
    rq, rk, rv, rg = jnp.split(r_in, 4, axis=-1)
    shp = (B, S, RET_HEADS, HEAD_DIM)
    q = _rope(rq.reshape(shp)).transpose(0, 2, 1, 3)
    k = (_rope(rk.reshape(shp)) * HEAD_DIM ** -0.5).transpose(0, 2, 1, 3)
    v = rv.reshape(shp).transpose(0, 2, 1, 3)
    log_g = -jnp.exp(log_decay_param.astype(F32))
    y_fwd = _retention_scan(q, k, v, log_g[0], False)
    rev = lambda t: t[:, :, ::-1]
    y_bwd = rev(_retention_scan(rev(q), rev(k), rev(v), log_g[1], True))
    y = (y_fwd + y_bwd).transpose(0, 2, 1, 3)
    y = y * lax.rsqrt(jnp.mean(y * y, axis=-1, keepdims=True) + EPS)
    y = y.reshape(B, S, GROUP_WIDTH) * gn_gain.astype(F32)
    return jax.nn.silu(rg) * y


def _band_attention(q, k, v, radius):
    N, L, hd = q.shape
    blk = radius
    nb = -(-L // blk)
    pad = nb * blk - L
    qb = jnp.pad(q, ((0, 0), (0, pad), (0, 0))).reshape(N, nb, blk, hd)
    kp = jnp.pad(k, ((0, 0), (blk, blk + pad), (0, 0))).reshape(N, nb + 2, blk, hd)
    vp = jnp.pad(v, ((0, 0), (blk, blk + pad), (0, 0))).reshape(N, nb + 2, blk, hd)
    kb = jnp.concatenate([kp[:, :-2], kp[:, 1:-1], kp[:, 2:]], axis=2)
    vb = jnp.concatenate([vp[:, :-2], vp[:, 1:-1], vp[:, 2:]], axis=2)
    qi = (jnp.arange(nb)[:, None] * blk + jnp.arange(blk)[None, :])[:, :, None]
    kj = (jnp.arange(nb)[:, None] * blk - blk + jnp.arange(3 * blk)[None, :])[:, None, :]
    valid = (jnp.abs(kj - qi) <= radius) & (kj >= 0) & (kj < L)
    s = jnp.where(valid[None], jnp.einsum('nbqd,nbkd->nbqk', qb, kb), NEG)
    m = jnp.max(s, axis=-1, keepdims=True)
    p = jnp.exp(s - m)
    den = jnp.sum(p, axis=-1, keepdims=True)
    o = jnp.einsum('nbqk,nbkd->nbqd', p, vb) / den
    lse = (m + jnp.log(den))[..., 0]
    return o.reshape(N, nb * blk, hd)[:, :L], lse.reshape(N, nb * blk)[:, :L]


def _dilated_attention(q, k, v):
    B, S, H, hd = q.shape
    outs, lses = [], []
    for window, dil in DIL_BRANCHES:
        Ls = S // dil

        def to_sub(t, Ls=Ls, dil=dil):
            return t.reshape(B, Ls, dil, H, hd).transpose(0, 2, 3, 1, 4).reshape(B * dil * H, Ls, hd)

        o, lse = _band_attention(to_sub(q), to_sub(k), to_sub(v), window // (2 * dil))
        outs.append(o.reshape(B, dil, H, Ls, hd).transpose(0, 3, 1, 2, 4).reshape(B, S, H, hd))
        lses.append(lse.reshape(B, dil, H, Ls).transpose(0, 3, 1, 2).reshape(B, S, H))
    wts = jax.nn.softmax(jnp.stack(lses), axis=0)
    return jnp.sum(wts[..., None] * jnp.stack(outs), axis=0)


def _peer(xn, w_q, sub_keys, u, v):
    B, S, D = xn.shape
    T = B * S
    E = u.shape[0]
    xt = xn.reshape(T, D).astype(F32)
    experts, gates = _peer_route(xt, w_q, sub_keys)
    uv = jnp.concatenate([u.astype(F32).reshape(E, SUBLANES, LANES),
                          v.astype(F32).reshape(E, SUBLANES, LANES)], axis=1)
    out = _peer_eval(xt, experts, gates, uv)
    return out.reshape(B, S, D)


def kernel(x, mix_norm_gain, ffn_norm_gain, ev_w_in, ev_w_out, hy_conv_w, hy_conv_b, hy_fw1, hy_fb1, hy_fw2, hy_fb2, hy_fw3, hy_fb3, hy_fw_out, hy_freq, hy_skip, ret_log_decay, ret_gn_gain, od_w_in, od_w_out, na_qk_gain, na_rpb, dil_qk_gain, peer_w_q, peer_sub_keys, peer_u, peer_v):
    B, S, _ = x.shape
    depth = mix_norm_gain.shape[0]
    scale = HEAD_DIM ** -0.5
    for layer in range(depth):
        h = _rms_norm(x, mix_norm_gain[layer])
        if layer % 2 == 0:
            e = layer // 2
            proj = (h @ ev_w_in[e]).astype(F32)
            a_out = _hyena(proj[..., :HY_IN], hy_conv_w[e], hy_conv_b[e], hy_fw1[e], hy_fb1[e],
                           hy_fw2[e], hy_fb2[e], hy_fw3[e], hy_fb3[e], hy_fw_out[e], hy_freq[e], hy_skip[e])
            b_out = _retention(proj[..., HY_IN:], ret_log_decay[e], ret_gn_gain[e])
            mixed = jnp.concatenate([a_out, b_out], axis=-1).astype(x.dtype) @ ev_w_out[e]
        else:
            o = layer // 2
            proj = (h @ od_w_in[o]).astype(F32)
            na_w = 3 * NA_HEADS * HEAD_DIM
            slabs = proj[..., :na_w].reshape(B * S, na_w // LANES, LANES).transpose(1, 0, 2)
            c_slabs = _neighbourhood_attention(slabs, na_qk_gain[o, 0], na_qk_gain[o, 1], na_rpb[o], B, S)
            c_out = c_slabs.transpose(1, 2, 0, 3).reshape(B, S, NA_HEADS * HEAD_DIM)
            dl_q, dl_k, dl_v = jnp.split(proj[..., na_w:], 3, axis=-1)
            dshp = (B, S, DIL_HEADS, HEAD_DIM)
            dq = _rope(_head_rms(dl_q.reshape(dshp), dil_qk_gain[o, 0])) * scale
            dk = _rope(_head_rms(dl_k.reshape(dshp), dil_qk_gain[o, 1]))
            d_out = _dilated_attention(dq, dk, dl_v.reshape(dshp))
            mixed = jnp.concatenate([c_out, d_out.reshape(B, S, -1)], axis=-1).astype(x.dtype) @ od_w_out[o]
        x = x + mixed.astype(x.dtype)
        x = x + _peer(_rms_norm(x, ffn_norm_gain[layer]), peer_w_q[layer], peer_sub_keys[layer],
                      peer_u[layer], peer_v[layer]).astype(x.dtype)
    return x
```

```python
import functools
import math

import numpy as np
import jax
import jax.numpy as jnp
from jax import lax
from jax.experimental import pallas as pl
from jax.experimental.pallas import tpu as pltpu

F32 = jnp.float32
BF16 = jnp.bfloat16

D_MODEL = 1024
HEAD_DIM = 64
GROUP_WIDTH = D_MODEL // 2
EPS = 1e-6
NEG = -1e30
NEG_INF = float("-inf")
ROPE_THETA = 10000.0

HY_CHANNELS = GROUP_WIDTH
HY_ORDER = 2
HY_EMB = 33
HY_EMB_PAD = 40
HY_BANDS = (HY_EMB - 1) // 2
HY_HID = 64
HY_FAST_DECAY = 0.3
HY_SLOW_DECAY = 1.5
HY_DECAY_TARGET = 1e-2
HY_MAX_DECAY = math.log(HY_DECAY_TARGET) / HY_FAST_DECAY
HY_MIN_DECAY = math.log(HY_DECAY_TARGET) / HY_SLOW_DECAY
HY_MOD_SHIFT = 0.05

GRID_W = 64
NA_HEADS = GROUP_WIDTH // HEAD_DIM
NA_ROWS = 8
NA_COLS = 16

DIL_BRANCHES = ((128, 1), (512, 4), (2048, 16))
DIL_BLK = 64

PEER_HEADS = 8
PEER_KEYS = 128
PEER_DK = 128
PEER_TOPK = 16
PEER_SEL = PEER_HEADS * PEER_TOPK

SUBLANES = 8
LANES = 128
assert D_MODEL == SUBLANES * LANES
HEAD_PAIRS = GROUP_WIDTH // LANES
PEER_TOK = 8
ROUTE_TOK = 256
DMA_THREADS = 2
MM_TOK = 256
MM_NCHUNK = 512
RET_BQ = 256
HY_CB = 128
HY_FILT_CB = 256
VMEM_LIMIT = 48 * 1024 * 1024
VMEM_LIMIT_HYENA = 56 * 1024 * 1024


def _split(x):
    hi = x.astype(BF16)
    return hi, (x - hi.astype(F32)).astype(BF16)


def _dot3(a, b):
    ah, al = _split(a)
    bh, bl = _split(b)
    d = lambda x, y: jnp.dot(x, y, preferred_element_type=F32)
    return d(ah, bh) + (d(ah, bl) + d(al, bh))


def _rms_rows(x, gain):
    return x * lax.rsqrt(jnp.mean(x * x, axis=-1, keepdims=True) + EPS) * gain


def _pair_sums(x):
    half = (lax.broadcasted_iota(jnp.int32, (LANES, LANES), 0) // HEAD_DIM ==
            lax.broadcasted_iota(jnp.int32, (LANES, LANES), 1) // HEAD_DIM).astype(BF16)
    hi, lo = _split(x)
    return (jnp.dot(hi, half, preferred_element_type=F32) +
            jnp.dot(lo, half, preferred_element_type=F32))


def _head_rms2(t, gain2):
    ms = _pair_sums(t * t) * (1.0 / HEAD_DIM)
    return t * lax.rsqrt(ms + EPS) * gain2


def _rope_tables(S):
    lane = np.arange(LANES)
    inv = ROPE_THETA ** (-jnp.arange(0, HEAD_DIM, 2, dtype=F32) / HEAD_DIM)
    ang = jnp.arange(S, dtype=F32)[:, None] * inv[None, :]
    cos = jnp.tile(jnp.cos(ang), (1, LANES // (HEAD_DIM // 2)))
    sin = jnp.tile(jnp.sin(ang), (1, LANES // (HEAD_DIM // 2)))
    sign = np.where((lane % HEAD_DIM) < HEAD_DIM // 2, -1.0, 1.0).astype(np.float32)
    return cos, sin * sign[None, :]


def _rope2(t, cos, sin_signed):
    lane = lax.broadcasted_iota(jnp.int32, t.shape, 1)
    lower = (lane % HEAD_DIM) < HEAD_DIM // 2
    partner = jnp.where(lower, pltpu.roll(t, LANES - HEAD_DIM // 2, 1), pltpu.roll(t, HEAD_DIM // 2, 1))
    return t * cos + partner * sin_signed


def _mm_in_kernel(x_ref, g_ref, w_ref, o_ref):
    xn = _rms_rows(x_ref[...], g_ref[...]).astype(BF16)
    per = MM_NCHUNK // LANES
    for c in range(o_ref.shape[0] // per):
        y = jnp.dot(xn, w_ref[:, c * MM_NCHUNK:(c + 1) * MM_NCHUNK], preferred_element_type=F32)
        for s in range(per):
            o_ref[c * per + s] = y[:, s * LANES:(s + 1) * LANES]


def _norm_in_proj(x, gain, w):
    T, K = x.shape
    N = w.shape[1]
    assert N % MM_NCHUNK == 0
    return pl.pallas_call(
        _mm_in_kernel, grid=(T // MM_TOK,),
        in_specs=[pl.BlockSpec((MM_TOK, K), lambda i: (i, 0)),
                  pl.BlockSpec((1, K), lambda i: (0, 0)),
                  pl.BlockSpec((K, N), lambda i: (0, 0))],
        out_specs=pl.BlockSpec((N // LANES, MM_TOK, LANES), lambda i: (0, i, 0)),
        out_shape=jax.ShapeDtypeStruct((N // LANES, T, LANES), F32),
        compiler_params=pltpu.CompilerParams(dimension_semantics=("arbitrary",),
                                             vmem_limit_bytes=VMEM_LIMIT),
        name="norm_in_proj")(x, gain.astype(F32).reshape(1, K), w.astype(BF16))


def _mm_out_kernel(a_ref, b_ref, w_ref, x_ref, g_ref, xo_ref, xn_ref):
    lhs = jnp.concatenate([a_ref[s].astype(BF16) for s in range(a_ref.shape[0])] +
                          [b_ref[s].astype(BF16) for s in range(b_ref.shape[0])], axis=-1)
    xo = x_ref[...] + jnp.dot(lhs, w_ref[...], preferred_element_type=F32)
    xo_ref[...] = xo
    xn_ref[...] = _rms_rows(xo, g_ref[...])


def _out_proj_residual(a_slabs, b_slabs, w, x, gain):
    NA_, T, _ = a_slabs.shape
    NB_ = b_slabs.shape[0]
    N = w.shape[1]
    row = pl.BlockSpec((MM_TOK, N), lambda i: (i, 0))
    return pl.pallas_call(
        _mm_out_kernel, grid=(T // MM_TOK,),
        in_specs=[pl.BlockSpec((NA_, MM_TOK, LANES), lambda i: (0, i, 0)),
                  pl.BlockSpec((NB_, MM_TOK, LANES), lambda i: (0, i, 0)),
                  pl.BlockSpec(((NA_ + NB_) * LANES, N), lambda i: (0, 0)),
                  row,
                  pl.BlockSpec((1, N), lambda i: (0, 0))],
        out_specs=[row, row],
        out_shape=[jax.ShapeDtypeStruct((T, N), F32), jax.ShapeDtypeStruct((T, N), F32)],
        compiler_params=pltpu.CompilerParams(dimension_semantics=("arbitrary",),
                                             vmem_limit_bytes=VMEM_LIMIT),
        name="out_proj_residual")(a_slabs, b_slabs, w.astype(BF16), x, gain.astype(F32).reshape(1, N))


def _dft_tables(L):
    N = 2 * L
    NF = -(-(L + 1) // LANES) * LANES
    k = jnp.arange(NF, dtype=jnp.int32)[:, None]
    n = jnp.arange(L, dtype=jnp.int32)[None, :]
    ang = ((k * n) % N).astype(F32) * (2.0 * math.pi / N)
    live = k <= L
    fr = jnp.where(live, jnp.cos(ang), 0.0).astype(BF16)
    fi = jnp.where(live, -jnp.sin(ang), 0.0).astype(BF16)
    kk = np.arange(NF)
    wk = np.where((kk == 0) | (kk == L), 1.0, np.where(kk < L, 2.0, 0.0)) / N
    return fr, fi, jnp.asarray(wk.astype(np.float32).reshape(NF, 1))


def _filter_consts(L):
    t_norm = jnp.linspace(0.0, 1.0, L, dtype=F32)[:, None]
    w = 2.0 * math.pi * jnp.arange(L, dtype=F32)[:, None] / L
    f = jnp.linspace(1e-4, HY_BANDS - 1, HY_BANDS, dtype=F32)[None, :]
    emb = jnp.concatenate([t_norm, jnp.cos(f * w), -jnp.sin(f * w)], axis=-1)
    emb = jnp.pad(emb, ((0, 0), (0, HY_EMB_PAD - HY_EMB)))
    deltas = jnp.abs(jnp.linspace(HY_MIN_DECAY, HY_MAX_DECAY, HY_CHANNELS, dtype=F32))
    window = jnp.exp(-t_norm * deltas[None, :]) + HY_MOD_SHIFT
    return emb, window


def _filt_kernel(emb_ref, w1_ref, b1_ref, w2_ref, b2_ref, w3_ref, b3_ref, wo_ref, freq_ref, win_ref,
                 dr_ref, di_ref, hr_ref, hi_ref):
    fr = freq_ref[...]
    h = jnp.sin(fr * (_dot3(emb_ref[...], w1_ref[...]) + b1_ref[...]))
    h = jnp.sin(fr * (_dot3(h, w2_ref[...]) + b2_ref[...]))
    h = jnp.sin(fr * (_dot3(h, w3_ref[...]) + b3_ref[...]))
    win = win_ref[...]
    fwd = _dot3(h, wo_ref[0, 0]) * win
    bwd = _dot3(h, wo_ref[0, 1]) * win
    row = lax.broadcasted_iota(jnp.int32, bwd.shape, 0)
    bwd = jnp.where(row == 0, 0.0, bwd)
    sh, sl = _split(fwd + bwd)
    dh, dl = _split(fwd - bwd)
    d = lambda x, y: jnp.dot(x, y, preferred_element_type=F32)
    hr_ref[0] = d(dr_ref[...], sh) + d(dr_ref[...], sl)
    hi_ref[0] = d(di_ref[...], dh) + d(di_ref[...], dl)


def _hyena_filter_spectrum(fw1, fb1, fw2, fb2, fw3, fb3, fw_out, freq, emb, window, dr, di):
    L = emb.shape[0]
    NF = dr.shape[0]
    C = HY_CHANNELS
    w1 = jnp.pad(fw1.astype(F32), ((0, HY_EMB_PAD - HY_EMB), (0, 0)))
    wo = fw_out.astype(F32).reshape(HY_HID, HY_ORDER, 2, C).transpose(1, 2, 0, 3)
    vec = lambda v: v.astype(F32).reshape(1, HY_HID)
    full = lambda a: pl.BlockSpec(a.shape, lambda o, c: (0,) * a.ndim)
    args = (emb, w1, vec(fb1), fw2.astype(F32), vec(fb2), fw3.astype(F32), vec(fb3))
    once = pl.Buffered(1)
    out = pl.BlockSpec((1, NF, HY_FILT_CB), lambda o, c: (o, 0, c))
    return pl.pallas_call(
        _filt_kernel, grid=(HY_ORDER, C // HY_FILT_CB),
        in_specs=[full(a) for a in args] + [
            pl.BlockSpec((1, 2, HY_HID, HY_FILT_CB), lambda o, c: (o, 0, 0, c)),
            pl.BlockSpec((1, HY_HID), lambda o, c: (0, 0)),
            pl.BlockSpec((L, HY_FILT_CB), lambda o, c: (0, c)),
            pl.BlockSpec((NF, L), lambda o, c: (0, 0), pipeline_mode=once),
            pl.BlockSpec((NF, L), lambda o, c: (0, 0), pipeline_mode=once)],
        out_specs=[out, out],
        out_shape=[jax.ShapeDtypeStruct((HY_ORDER, NF, C), F32)] * 2,
        compiler_params=pltpu.CompilerParams(dimension_semantics=("arbitrary", "arbitrary"),
                                             vmem_limit_bytes=VMEM_LIMIT_HYENA),
        name="hyena_filter")(*args, wo, vec(freq), window, dr, di)


def _conv_kernel(v_ref, x1_ref, x2_ref, cw_ref, cb_ref, skip_ref, hr_ref, hi_ref, dr_ref, di_ref, wk_ref, o_ref):
    S = v_ref.shape[2]
    nsl = v_ref.shape[0]
    row = lax.broadcasted_iota(jnp.int32, (S, LANES), 0)

    def short(u_ref, g):
        cols = []
        for s in range(nsl):
            u = u_ref[s, 0]
            w = cw_ref[g, s]
            up = jnp.where(row == 0, 0.0, pltpu.roll(u, 1, 0))
            un = jnp.where(row == S - 1, 0.0, pltpu.roll(u, S - 1, 0))
            cols.append(up * w[0:1] + u * w[1:2] + un * w[2:3] + cb_ref[g, s])
        return jnp.concatenate(cols, axis=-1)

    d = lambda x, y: jnp.dot(x, y, preferred_element_type=F32)
    wk = wk_ref[...]

    def long_conv(z, o):
        zb = z.astype(BF16)
        zr = d(dr_ref[...], zb)
        zi = d(di_ref[...], zb)
        hr = hr_ref[o]
        hi = hi_ref[o]
        yr = ((zr * hr - zi * hi) * wk).T.astype(BF16)
        yi = ((zr * hi + zi * hr) * wk).T.astype(BF16)
        y = d(yr, dr_ref[...]) + d(yi, di_ref[...])
        return y.T + z * skip_ref[o]

    v = short(v_ref, 0)
    z = short(x1_ref, 1) * long_conv(v, 0)
    out = short(x2_ref, 2) * long_conv(z, 1)
    for s in range(nsl):
        o_ref[s, 0] = out[:, s * LANES:(s + 1) * LANES]


def _hyena_conv(slabs, conv_w, conv_b, skip, hr, hi, dr, di, wk, B, S):
    C = HY_CHANNELS
    NF = dr.shape[0]
    nsl = HY_CB // LANES
    nc = C // HY_CB
    x = slabs.reshape(slabs.shape[0], B, S, LANES)
    cw = conv_w.astype(F32).reshape(3, 3, C // LANES, LANES).transpose(1, 2, 0, 3)
    cb = conv_b.astype(F32).reshape(3, C // LANES, 1, LANES)
    sk = skip.astype(F32).reshape(HY_ORDER, 1, C)
    spec = lambda g: pl.BlockSpec((nsl, 1, S, LANES), lambda c, b, g=g: (g * nc + c, b, 0, 0))
    once = pl.Buffered(1)
    return pl.pallas_call(
        _conv_kernel, grid=(nc, B),
        in_specs=[spec(0), spec(1), spec(2),
                  pl.BlockSpec((3, nsl, 3, LANES), lambda c, b: (0, c, 0, 0)),
                  pl.BlockSpec((3, nsl, 1, LANES), lambda c, b: (0, c, 0, 0)),
                  pl.BlockSpec((HY_ORDER, 1, HY_CB), lambda c, b: (0, 0, c)),
                  pl.BlockSpec((HY_ORDER, NF, HY_CB), lambda c, b: (0, 0, c), pipeline_mode=once),
                  pl.BlockSpec((HY_ORDER, NF, HY_CB), lambda c, b: (0, 0, c), pipeline_mode=once),
                  pl.BlockSpec((NF, S), lambda c, b: (0, 0), pipeline_mode=once),
                  pl.BlockSpec((NF, S), lambda c, b: (0, 0), pipeline_mode=once),
                  pl.BlockSpec((NF, 1), lambda c, b: (0, 0))],
        out_specs=pl.BlockSpec((nsl, 1, S, LANES), lambda c, b: (c, b, 0, 0)),
        out_shape=jax.ShapeDtypeStruct((C // LANES, B, S, LANES), F32),
        compiler_params=pltpu.CompilerParams(dimension_semantics=("arbitrary", "arbitrary"),
                                             vmem_limit_bytes=VMEM_LIMIT_HYENA),
        name="hyena_conv")(x, x, x, cw, cb, sk, hr, hi, dr, di, wk)


def _ret_kernel(lg_ref, q_ref, k_ref, v_ref, g_ref, cos_ref, sin_ref, gn_ref, o_ref, qr_ref, kr_ref, vb_ref):
    hp = pl.program_id(0)
    S = q_ref.shape[1]
    cos = cos_ref[...]
    sin = sin_ref[...]
    qr_ref[...] = _rope2(q_ref[0], cos, sin).astype(BF16)
    kr_ref[...] = (_rope2(k_ref[0], cos, sin) * HEAD_DIM ** -0.5).astype(BF16)
    vb_ref[...] = v_ref[0].astype(BF16)
    first = lax.broadcasted_iota(jnp.int32, (RET_BQ, LANES), 1) < HEAD_DIM
    m_idx = lax.broadcasted_iota(jnp.int32, (RET_BQ, S), 1)
    n_loc = lax.broadcasted_iota(jnp.int32, (RET_BQ, S), 0)

    def body(r, c):
        row0 = pl.multiple_of(r * RET_BQ, RET_BQ)
        diff = (n_loc + row0 - m_idx).astype(F32)
        q2 = qr_ref[pl.ds(row0, RET_BQ), :]
        ys = []
        for hh in range(2):
            lgf = lg_ref[0, 2 * hp + hh]
            lgb = lg_ref[1, 2 * hp + hh]
            dec = jnp.exp(diff * jnp.where(diff >= 0.0, lgf, -lgb))
            qm = jnp.where(first if hh == 0 else ~first, q2, jnp.zeros_like(q2))
            s = lax.dot_general(qm, kr_ref[...], (((1,), (1,)), ((), ())), preferred_element_type=F32)
            ys.append(jnp.dot((s * dec).astype(BF16), vb_ref[...], preferred_element_type=F32))
        y = jnp.where(first, ys[0], ys[1])
        y = y * lax.rsqrt(_pair_sums(y * y) * (1.0 / HEAD_DIM) + EPS) * gn_ref[0]
        g = g_ref[0, pl.ds(row0, RET_BQ), :]
        o_ref[0, pl.ds(row0, RET_BQ), :] = g * jax.nn.sigmoid(g) * y
        return c

    lax.fori_loop(0, S // RET_BQ, body, 0)


def _retention(slabs, first_slab, log_decay_param, gn_gain, cos, sin, B, S):
    HP = HEAD_PAIRS
    x = slabs.reshape(slabs.shape[0], B, S, LANES)
    lg = -jnp.exp(log_decay_param.astype(F32))
    gn = gn_gain.astype(F32).reshape(HP, 1, LANES)
    spec = lambda g: pl.BlockSpec((None, 1, S, LANES), lambda h, b, g=g: (first_slab + g * HP + h, b, 0, 0))
    tab = pl.BlockSpec((S, LANES), lambda h, b: (0, 0))
    return pl.pallas_call(
        _ret_kernel, grid=(HP, B),
        in_specs=[pl.BlockSpec(memory_space=pltpu.SMEM), spec(0), spec(1), spec(2), spec(3), tab, tab,
                  pl.BlockSpec((None, 1, LANES), lambda h, b: (h, 0, 0))],
        out_specs=pl.BlockSpec((None, 1, S, LANES), lambda h, b: (h, b, 0, 0)),
        out_shape=jax.ShapeDtypeStruct((HP, B, S, LANES), F32),
        scratch_shapes=[pltpu.VMEM((S, LANES), BF16)] * 3,
        compiler_params=pltpu.CompilerParams(dimension_semantics=("arbitrary", "arbitrary"),
                                             vmem_limit_bytes=VMEM_LIMIT),
        name="retention")(lg, x, x, x, x, cos, sin, gn)


def _na_kernel(q_ref, k_ref, v_ref, gq_ref, gk_ref, bias_ref, o_ref, qn_ref, kn_ref, *, rows):
    scale = HEAD_DIM ** -0.5
    qn_ref[...] = (_head_rms2(q_ref[0], gq_ref[...]) * scale).astype(BF16)
    kn_ref[...] = _head_rms2(k_ref[0], gk_ref[...]).astype(BF16)
    lane = lax.broadcasted_iota(jnp.int32, (GRID_W, LANES), 1)
    first = lane < HEAD_DIM
    half_rows = NA_ROWS // 2

    def body(r, c):
        r0 = jnp.clip(r - half_rows, 0, rows - NA_ROWS)
        pat = jnp.where(r < half_rows, r,
                        jnp.where(r > rows - half_rows, r - (rows - NA_ROWS), half_rows))
        q2 = qn_ref[pl.ds(pl.multiple_of(r * GRID_W, GRID_W), GRID_W), :]
        kstart = pl.multiple_of(r0 * GRID_W, GRID_W)
        k2 = kn_ref[pl.ds(kstart, NA_ROWS * GRID_W), :]
        v2 = v_ref[0, pl.ds(kstart, NA_ROWS * GRID_W), :].astype(BF16)
        outs = []
        for hh in range(2):
            qm = jnp.where(first if hh == 0 else ~first, q2, jnp.zeros_like(q2))
            s = lax.dot_general(qm, k2, (((1,), (1,)), ((), ())), preferred_element_type=F32)
            s = s + bias_ref[hh, pat]
            m = jnp.max(s, axis=-1, keepdims=True)
            e = jnp.exp(s - m)
            l = jnp.sum(e, axis=-1, keepdims=True)
            outs.append(jnp.dot(e.astype(BF16), v2, preferred_element_type=F32) / l)
        o_ref[0, pl.ds(pl.multiple_of(r * GRID_W, GRID_W), GRID_W), :] = jnp.where(first, outs[0], outs[1])
        return c

    lax.fori_loop(0, rows, body, 0)


def _na_bias_table(rpb, rows):
    kh = NA_ROWS
    c = np.arange(GRID_W)
    c0 = np.clip(c - NA_COLS // 2, 0, GRID_W - NA_COLS)
    col_valid = (c[None, :] >= c0[:, None]) & (c[None, :] < c0[:, None] + NA_COLS)
    col_off = np.clip(c[None, :] - c[:, None] + NA_COLS - 1, 0, 2 * NA_COLS - 2)
    pat_rows = list(range(kh // 2)) + [kh // 2] + list(range(rows - kh // 2 + 1, rows))
    tabs = []
    for r in pat_rows:
        r0 = min(max(r - kh // 2, 0), rows - kh)
        ro = r0 + np.arange(kh) - r + NA_ROWS - 1
        b = rpb.astype(F32)[:, ro][:, :, col_off]
        b = jnp.where(col_valid[None, None], b, NEG)
        tabs.append(b.transpose(0, 2, 1, 3).reshape(rpb.shape[0], GRID_W, kh * GRID_W))
    return jnp.stack(tabs, axis=1)


def _neighbourhood_attention(slabs, gq, gk, rpb, B, S):
    HP = HEAD_PAIRS
    rows = S // GRID_W
    assert rows >= NA_ROWS
    n_pat = NA_ROWS
    bias = _na_bias_table(rpb, rows).reshape(HP, 2, n_pat, GRID_W, NA_ROWS * GRID_W)
    g2q = jnp.tile(gq.astype(F32), 2).reshape(1, LANES)
    g2k = jnp.tile(gk.astype(F32), 2).reshape(1, LANES)
    x = slabs.reshape(slabs.shape[0], B, S, LANES)
    spec = lambda g: pl.BlockSpec((None, 1, S, LANES), lambda h, b, g=g: (g * HP + h, b, 0, 0))
    vec = pl.BlockSpec((1, LANES), lambda h, b: (0, 0))
    return pl.pallas_call(
        functools.partial(_na_kernel, rows=rows), grid=(HP, B),
        in_specs=[spec(0), spec(1), spec(2), vec, vec,
                  pl.BlockSpec((None, 2, n_pat, GRID_W, NA_ROWS * GRID_W), lambda h, b: (h, 0, 0, 0, 0))],
        out_specs=pl.BlockSpec((None, 1, S, LANES), lambda h, b: (h, b, 0, 0)),
        out_shape=jax.ShapeDtypeStruct((HP, B, S, LANES), F32),
        scratch_shapes=[pltpu.VMEM((S, LANES), BF16), pltpu.VMEM((S, LANES), BF16)],
        compiler_params=pltpu.CompilerParams(dimension_semantics=("arbitrary", "arbitrary"),
                                             vmem_limit_bytes=VMEM_LIMIT),
        name="nbr_attention")(x, x, x, g2q, g2k, bias)


def _dil_kernel(q_ref, k_ref, v_ref, gq_ref, gk_ref, cos_ref, sin_ref, o_ref,
                qn_ref, kn_ref, qd_ref, kd_ref, vd_ref, od_ref, ld_ref, os_ref, ls_ref):
    S = q_ref.shape[1]
    BLK = DIL_BLK
    nblk = S // BLK
    cos = cos_ref[...]
    sin = sin_ref[...]
    qn_ref[...] = _rope2(_head_rms2(q_ref[0], gq_ref[...]), cos, sin) * HEAD_DIM ** -0.5
    kn_ref[...] = _rope2(_head_rms2(k_ref[0], gk_ref[...]), cos, sin)
    first = lax.broadcasted_iota(jnp.int32, (BLK, LANES), 1) < HEAD_DIM
    qi = lax.broadcasted_iota(jnp.int32, (BLK, 3 * BLK), 0)
    kc = lax.broadcasted_iota(jnp.int32, (BLK, 3 * BLK), 1)
    band = jnp.abs(kc - BLK - qi) <= BLK
    for bi, (window, dil) in enumerate(DIL_BRANCHES):
        assert window // (2 * dil) == BLK
        Ls = S // dil
        nbc = Ls // BLK
        for r in range(dil):
            sl = pl.ds(r, Ls, stride=dil) if dil > 1 else pl.ds(0, Ls)
            qd_ref[pl.ds(r * Ls, Ls), :] = qn_ref[sl, :].astype(BF16)
            kd_ref[pl.ds(r * Ls, Ls), :] = kn_ref[sl, :].astype(BF16)
            vd_ref[pl.ds(r * Ls, Ls), :] = v_ref.at[0][sl, :].astype(BF16)

        def body(fb, c):
            j = lax.rem(fb, nbc)
            row0 = pl.multiple_of(fb * BLK, BLK)
            prev0 = pl.multiple_of(jnp.maximum(fb - 1, 0) * BLK, BLK)
            next0 = pl.multiple_of(jnp.minimum(fb + 1, nblk - 1) * BLK, BLK)
            kk = jnp.concatenate([kd_ref[pl.ds(prev0, BLK), :], kd_ref[pl.ds(row0, BLK), :],
                                  kd_ref[pl.ds(next0, BLK), :]], axis=0)
            vv = jnp.concatenate([vd_ref[pl.ds(prev0, BLK), :], vd_ref[pl.ds(row0, BLK), :],
                                  vd_ref[pl.ds(next0, BLK), :]], axis=0)
            valid = band & ((kc >= BLK) | (j > 0)) & ((kc < 2 * BLK) | (j < nbc - 1))
            q2 = qd_ref[pl.ds(row0, BLK), :]
            outs, lses = [], []
            for hh in range(2):
                qm = jnp.where(first if hh == 0 else ~first, q2, jnp.zeros_like(q2))
                s = lax.dot_general(qm, kk, (((1,), (1,)), ((), ())), preferred_element_type=F32)
                s = jnp.where(valid, s, NEG)
                m = jnp.max(s, axis=-1, keepdims=True)
                p = jnp.exp(s - m)
                den = jnp.sum(p, axis=-1, keepdims=True)
                outs.append(jnp.dot(p.astype(BF16), vv, preferred_element_type=F32) / den)
                lses.append(m + jnp.log(den))
            od_ref[pl.ds(row0, BLK), :] = jnp.where(first, outs[0], outs[1])
            ld_ref[pl.ds(row0, BLK), :] = jnp.where(first, lses[0], lses[1])
            return c

        lax.fori_loop(0, nblk, body, 0)
        for r in range(dil):
            sl = pl.ds(r, Ls, stride=dil) if dil > 1 else pl.ds(0, Ls)
            os_ref.at[bi][sl, :] = od_ref[pl.ds(r * Ls, Ls), :]
            ls_ref.at[bi][sl, :] = ld_ref[pl.ds(r * Ls, Ls), :]
    l0, l1, l2 = ls_ref[0], ls_ref[1], ls_ref[2]
    mx = jnp.maximum(jnp.maximum(l0, l1), l2)
    w0, w1, w2 = jnp.exp(l0 - mx), jnp.exp(l1 - mx), jnp.exp(l2 - mx)
    o_ref[0] = (w0 * os_ref[0] + w1 * os_ref[1] + w2 * os_ref[2]) / (w0 + w1 + w2)


def _dilated_attention(slabs, first_slab, gq, gk, cos, sin, B, S):
    HP = HEAD_PAIRS
    x = slabs.reshape(slabs.shape[0], B, S, LANES)
    g2q = jnp.tile(gq.astype(F32), 2).reshape(1, LANES)
    g2k = jnp.tile(gk.astype(F32), 2).reshape(1, LANES)
    spec = lambda g: pl.BlockSpec((None, 1, S, LANES), lambda h, b, g=g: (first_slab + g * HP + h, b, 0, 0))
    tab = pl.BlockSpec((S, LANES), lambda h, b: (0, 0))
    vec = pl.BlockSpec((1, LANES), lambda h, b: (0, 0))
    return pl.pallas_call(
        _dil_kernel, grid=(HP, B),
        in_specs=[spec(0), spec(1), spec(2), vec, vec, tab, tab],
        out_specs=pl.BlockSpec((None, 1, S, LANES), lambda h, b: (h, b, 0, 0)),
        out_shape=jax.ShapeDtypeStruct((HP, B, S, LANES), F32),
        scratch_shapes=[pltpu.VMEM((S, LANES), F32), pltpu.VMEM((S, LANES), F32),
                        pltpu.VMEM((S, LANES), BF16), pltpu.VMEM((S, LANES), BF16),
                        pltpu.VMEM((S, LANES), BF16),
                        pltpu.VMEM((S, LANES), F32), pltpu.VMEM((S, LANES), F32),
                        pltpu.VMEM((len(DIL_BRANCHES), S, LANES), F32),
                        pltpu.VMEM((len(DIL_BRANCHES), S, LANES), F32)],
        compiler_params=pltpu.CompilerParams(dimension_semantics=("arbitrary", "arbitrary"),
                                             vmem_limit_bytes=VMEM_LIMIT),
        name="dilated_attention")(x, x, x, g2q, g2k, cos, sin)


def _stair_tables():
    a_idx = [0] * 16 + [a for a in range(1, 8) for _ in range(8)] + list(range(8, 16))
    b_idx = list(range(16)) + [b for _ in range(1, 8) for b in range(8)] + [0] * 8
    a_idx = np.array(a_idx)
    b_idx = np.array(b_idx)
    valid = (a_idx + 1) * (b_idx + 1) <= PEER_TOPK
    return a_idx, b_idx, valid


_ST_A, _ST_B, _ST_VALID = _stair_tables()
N_CAND = len(_ST_A)


def _top16_rows(sc, iota):
    vals, idxs = [], []
    big = float(sc.shape[0])
    for r in range(PEER_TOPK):
        m = jnp.max(sc, axis=0, keepdims=True)
        idx = jnp.min(jnp.where(sc == m, iota, big), axis=0, keepdims=True)
        vals.append(m)
        idxs.append(idx)
        if r + 1 < PEER_TOPK:
            sc = jnp.where(iota == idx, NEG_INF, sc)
    return jnp.concatenate(vals, axis=0), jnp.concatenate(idxs, axis=0)


def _route_kernel(xn_ref, wqT_ref, keys_ref, flat_ref, e_ref, g_ref, qT_ref, eT_ref, gT_ref):
    Tb = xn_ref.shape[0]
    xb = xn_ref[...].astype(BF16)
    qT_ref[...] = lax.dot_general(wqT_ref[...], xb, (((1,), (1,)), ((), ())), preferred_element_type=F32)
    iota = lax.broadcasted_iota(jnp.int32, (PEER_KEYS, Tb), 0).astype(F32)
    flat = jnp.broadcast_to(flat_ref[...], (N_CAND, Tb))
    for h in range(PEER_HEADS):
        s, ix = [], []
        for p in range(2):
            r0 = (h * 2 + p) * (PEER_DK // 2)
            sc = jnp.dot(keys_ref[h * 2 + p], qT_ref[pl.ds(r0, PEER_DK // 2), :].astype(BF16),
                         preferred_element_type=F32)
            sp, ip = _top16_rows(sc, iota)
            s.append(sp)
            ix.append(ip)
        s0, s1 = s
        i0, i1 = ix[0] * float(PEER_KEYS), ix[1]
        parts_s = [s0[0:1] + s1]
        parts_e = [i0[0:1] + i1]
        for a in range(1, 8):
            parts_s.append(s0[a:a + 1] + s1[0:8])
            parts_e.append(i0[a:a + 1] + i1[0:8])
        parts_s.append(s0[8:16] + s1[0:1])
        parts_e.append(i0[8:16] + i1[0:1])
        cand = jnp.where(flat < 256.0, jnp.concatenate(parts_s, axis=0), NEG_INF)
        eid = jnp.concatenate(parts_e, axis=0)
        gs, es = [], []
        for r in range(PEER_TOPK):
            m = jnp.max(cand, axis=0, keepdims=True)
            fsel = jnp.min(jnp.where(cand == m, flat, float(1 << 20)), axis=0, keepdims=True)
            hit = flat == fsel
            es.append(jnp.max(jnp.where(hit, eid, -1.0), axis=0, keepdims=True))
            gs.append(m)
            if r + 1 < PEER_TOPK:
                cand = jnp.where(hit, NEG_INF, cand)
        g = jnp.concatenate(gs, axis=0)
        ex = jnp.exp(g - g[0:1])
        eT_ref[pl.ds(h * PEER_TOPK, PEER_TOPK), :] = jnp.concatenate(es, axis=0).astype(jnp.int32)
        gT_ref[pl.ds(h * PEER_TOPK, PEER_TOPK), :] = ex / jnp.sum(ex, axis=0, keepdims=True)
    e_ref[...] = eT_ref[...].T
    g_ref[...] = gT_ref[...].T


def _peer_route(xn, w_q, sub_keys):
    T = xn.shape[0]
    Tb = ROUTE_TOK
    wqT = w_q.T.astype(BF16)
    keys = sub_keys.reshape(PEER_HEADS * 2, PEER_KEYS, PEER_DK // 2).astype(BF16)
    flat = np.where(_ST_VALID, _ST_A * 16 + _ST_B, 1 << 16).astype(np.float32).reshape(N_CAND, 1)
    return pl.pallas_call(
        _route_kernel, grid=(T // Tb,),
        in_specs=[pl.BlockSpec((Tb, D_MODEL), lambda i: (i, 0)),
                  pl.BlockSpec((D_MODEL, D_MODEL), lambda i: (0, 0)),
                  pl.BlockSpec((PEER_HEADS * 2, PEER_KEYS, PEER_DK // 2), lambda i: (0, 0, 0)),
                  pl.BlockSpec((N_CAND, 1), lambda i: (0, 0))],
        out_specs=[pl.BlockSpec((Tb, PEER_SEL), lambda i: (i, 0)),
                   pl.BlockSpec((Tb, PEER_SEL), lambda i: (i, 0))],
        out_shape=[jax.ShapeDtypeStruct((T, PEER_SEL), jnp.int32),
                   jax.ShapeDtypeStruct((T, PEER_SEL), F32)],
        scratch_shapes=[pltpu.VMEM((D_MODEL, Tb), F32), pltpu.VMEM((PEER_SEL, Tb), jnp.int32),
                        pltpu.VMEM((PEER_SEL, Tb), F32)],
        compiler_params=pltpu.CompilerParams(dimension_semantics=("arbitrary",),
                                             vmem_limit_bytes=VMEM_LIMIT),
        name="peer_route")(xn, wqT, keys, jnp.asarray(flat))


def _peer_eval_kernel(idx_cur, idx_nxt, x_ref, r_ref, g_ref, uv_hbm, o_ref, buf, sem):
    i = pl.program_id(0)
    n = pl.num_programs(0)
    G = PEER_TOK

    def issue_tok(idx_ref, row, slot, g):
        for k in range(PEER_SEL):
            e = idx_ref[row, k]
            pltpu.make_async_copy(uv_hbm.at[e], buf.at[slot, g, k // SUBLANES, :, k % SUBLANES, :],
                                  sem.at[slot]).start(priority=k % DMA_THREADS)

    def wait_slot(slot):
        pltpu.make_async_copy(buf.at[slot], buf.at[slot], sem.at[slot]).wait()

    def compute_tok(slot, g, row, gates_t):
        xg = x_ref[row]
        acc = buf[slot, g, :, 0].reshape(PEER_SEL, LANES) * xg[0:1, :]
        for c in range(1, SUBLANES):
            acc = acc + buf[slot, g, :, c].reshape(PEER_SEL, LANES) * xg[c:c + 1, :]
        h = jnp.sum(acc, axis=-1, keepdims=True)
        a = jax.nn.gelu(h) * gates_t[:, row:row + 1]
        ab = jnp.broadcast_to(a, (PEER_SEL, LANES))
        rows = [jnp.sum(ab * buf[slot, g, :, SUBLANES + c].reshape(PEER_SEL, LANES), axis=0, keepdims=True)
                for c in range(SUBLANES)]
        o_ref[row] = r_ref[row] + jnp.concatenate(rows, axis=0)

    @pl.when(i == 0)
    def _():
        for g in range(G):
            issue_tok(idx_cur, g, 0, g)

    gates_t = g_ref[...].T
    wait_slot(0)
    for g in range(G):
        issue_tok(idx_cur, G + g, 1, g)
        compute_tok(0, g, g, gates_t)
    wait_slot(1)
    for g in range(G):
        issue_tok(idx_nxt, g, 0, g)
        compute_tok(1, g, G + g, gates_t)

    @pl.when(i == n - 1)
    def _():
        wait_slot(0)


def _peer_eval(xt, res, experts, gates, uv):
    T = xt.shape[0]
    G = PEER_TOK
    n = T // (2 * G)
    x3 = xt.reshape(T, SUBLANES, LANES)
    r3 = res.reshape(T, SUBLANES, LANES)
    tok = pl.BlockSpec((2 * G, SUBLANES, LANES), lambda i: (i, 0, 0))
    out = pl.pallas_call(
        _peer_eval_kernel,
        grid=(n,),
        in_specs=[
            pl.BlockSpec((2 * G, PEER_SEL), lambda i: (i, 0), memory_space=pltpu.SMEM),
            pl.BlockSpec((2 * G, PEER_SEL), lambda i: (jnp.minimum(i + 1, n - 1), 0),
                         memory_space=pltpu.SMEM),
            tok,
            tok,
            pl.BlockSpec((2 * G, PEER_SEL), lambda i: (i, 0)),
            pl.BlockSpec(memory_space=pl.ANY),
        ],
        out_specs=tok,
        out_shape=jax.ShapeDtypeStruct((T, SUBLANES, LANES), F32),
        scratch_shapes=[
            pltpu.VMEM((2, G, 2 * SUBLANES, 2 * SUBLANES, SUBLANES, LANES), F32),
            pltpu.SemaphoreType.DMA((2,)),
        ],
        compiler_params=pltpu.CompilerParams(
            dimension_semantics=("arbitrary",),
            vmem_limit_bytes=VMEM_LIMIT),
        name="peer_eval",
    )(experts, experts, x3, r3, gates, uv)
    return out.reshape(T, D_MODEL)


def _peer(xn, res, w_q, sub_keys, u, v):
    E = u.shape[0]
    experts, gates = _peer_route(xn, w_q, sub_keys)
    uv = jnp.concatenate([u.astype(F32).reshape(E, SUBLANES, LANES),
                          v.astype(F32).reshape(E, SUBLANES, LANES)], axis=1)
    return _peer_eval(xn, res, experts, gates, uv)


def kernel(x, mix_norm_gain, ffn_norm_gain, ev_w_in, ev_w_out, hy_conv_w, hy_conv_b, hy_fw1, hy_fb1, hy_fw2, hy_fb2, hy_fw3, hy_fb3, hy_fw_out, hy_freq, hy_skip, ret_log_decay, ret_gn_gain, od_w_in, od_w_out, na_qk_gain, na_rpb, dil_qk_gain, peer_w_q, peer_sub_keys, peer_u, peer_v):
    B, S, D = x.shape
    T = B * S
    depth = mix_norm_gain.shape[0]
    cos, sin = _rope_tables(S)
    emb, window = _filter_consts(S)
    dr, di, wk = _dft_tables(S)
    xt = x.reshape(T, D).astype(F32)
    for layer in range(depth):
        if layer % 2 == 0:
            e = layer // 2
            slabs = _norm_in_proj(xt, mix_norm_gain[layer], ev_w_in[e])
            hr, hi = _hyena_filter_spectrum(hy_fw1[e], hy_fb1[e], hy_fw2[e], hy_fb2[e], hy_fw3[e], hy_fb3[e],
                                            hy_fw_out[e], hy_freq[e], emb, window, dr, di)
            a_out = _hyena_conv(slabs, hy_conv_w[e], hy_conv_b[e], hy_skip[e], hr, hi, dr, di, wk, B, S)
            b_out = _retention(slabs, 3 * HEAD_PAIRS, ret_log_decay[e], ret_gn_gain[e], cos, sin, B, S)
            w_out = ev_w_out[e]
        else:
            o = layer // 2
            slabs = _norm_in_proj(xt, mix_norm_gain[layer], od_w_in[o])
            a_out = _neighbourhood_attention(slabs, na_qk_gain[o, 0], na_qk_gain[o, 1], na_rpb[o], B, S)
            b_out = _dilated_attention(slabs, 3 * HEAD_PAIRS, dil_qk_gain[o, 0], dil_qk_gain[o, 1],
                                       cos, sin, B, S)
            w_out = od_w_out[o]
        xt, xn = _out_proj_residual(a_out.reshape(HEAD_PAIRS, T, LANES), b_out.reshape(HEAD_PAIRS, T, LANES),
                                    w_out, xt, ffn_norm_gain[layer])
        xt = _peer(xn, xt, peer_w_q[layer], peer_sub_keys[layer], peer_u[layer], peer_v[layer])
    return xt.reshape(B, S, D).astype(x.dtype)
```

```python
import functools
import math

import numpy as np
import jax
import jax.numpy as jnp
from jax import lax
from jax.experimental import pallas as pl
from jax.experimental.pallas import tpu as pltpu

F32 = jnp.float32
BF16 = jnp.bfloat16

D_MODEL = 1024
HEAD_DIM = 64
GROUP_WIDTH = D_MODEL // 2
EPS = 1e-6
NEG = -1e30
NEG_INF = float("-inf")
ROPE_THETA = 10000.0

HY_CHANNELS = GROUP_WIDTH
HY_ORDER = 2
HY_EMB = 33
HY_EMB_PAD = 40
HY_BANDS = (HY_EMB - 1) // 2
HY_HID = 64
HY_FAST_DECAY = 0.3
HY_SLOW_DECAY = 1.5
HY_DECAY_TARGET = 1e-2
HY_MAX_DECAY = math.log(HY_DECAY_TARGET) / HY_FAST_DECAY
HY_MIN_DECAY = math.log(HY_DECAY_TARGET) / HY_SLOW_DECAY
HY_MOD_SHIFT = 0.05

GRID_W = 64
NA_HEADS = GROUP_WIDTH // HEAD_DIM
NA_ROWS = 8
NA_COLS = 16

DIL_BRANCHES = ((128, 1), (512, 4), (2048, 16))
DIL_BLK = 64
DIL_BQ = 256

PEER_HEADS = 8
PEER_KEYS = 128
PEER_DK = 128
PEER_TOPK = 16
PEER_SEL = PEER_HEADS * PEER_TOPK

SUBLANES = 8
LANES = 128
assert D_MODEL == SUBLANES * LANES
HEAD_PAIRS = GROUP_WIDTH // LANES
PEER_TOK = 8
ROUTE_TOK = 256
DMA_THREADS = 2
MM_TOK = 256
MM_NCHUNK = 512
RET_BQ = 256
HY_CB = 128
HY_FILT_CB = 256
VMEM_LIMIT = 48 * 1024 * 1024
VMEM_LIMIT_HYENA = 56 * 1024 * 1024


def _split(x):
    hi = x.astype(BF16)
    return hi, (x - hi.astype(F32)).astype(BF16)


def _dot3(a, b):
    ah, al = _split(a)
    bh, bl = _split(b)
    d = lambda x, y: jnp.dot(x, y, preferred_element_type=F32)
    return d(ah, bh) + (d(ah, bl) + d(al, bh))


def _rms_rows(x, gain):
    return x * lax.rsqrt(jnp.mean(x * x, axis=-1, keepdims=True) + EPS) * gain


def _pair_sums(x):
    half = (lax.broadcasted_iota(jnp.int32, (LANES, LANES), 0) // HEAD_DIM ==
            lax.broadcasted_iota(jnp.int32, (LANES, LANES), 1) // HEAD_DIM).astype(BF16)
    hi, lo = _split(x)
    return (jnp.dot(hi, half, preferred_element_type=F32) +
            jnp.dot(lo, half, preferred_element_type=F32))


def _head_rms2(t, gain2):
    ms = _pair_sums(t * t) * (1.0 / HEAD_DIM)
    return t * lax.rsqrt(ms + EPS) * gain2


def _rope_tables(S):
    lane = np.arange(LANES)
    inv = ROPE_THETA ** (-jnp.arange(0, HEAD_DIM, 2, dtype=F32) / HEAD_DIM)
    ang = jnp.arange(S, dtype=F32)[:, None] * inv[None, :]
    cos = jnp.tile(jnp.cos(ang), (1, LANES // (HEAD_DIM // 2)))
    sin = jnp.tile(jnp.sin(ang), (1, LANES // (HEAD_DIM // 2)))
    sign = np.where((lane % HEAD_DIM) < HEAD_DIM // 2, -1.0, 1.0).astype(np.float32)
    return cos, sin * sign[None, :]


def _rope2(t, cos, sin_signed):
    lane = lax.broadcasted_iota(jnp.int32, t.shape, 1)
    lower = (lane % HEAD_DIM) < HEAD_DIM // 2
    partner = jnp.where(lower, pltpu.roll(t, LANES - HEAD_DIM // 2, 1), pltpu.roll(t, HEAD_DIM // 2, 1))
    return t * cos + partner * sin_signed


def _mm_in_kernel(x_ref, g_ref, w_ref, o_ref):
    xn = _rms_rows(x_ref[...], g_ref[...]).astype(BF16)
    per = MM_NCHUNK // LANES
    for c in range(o_ref.shape[0] // per):
        y = jnp.dot(xn, w_ref[:, c * MM_NCHUNK:(c + 1) * MM_NCHUNK], preferred_element_type=F32)
        for s in range(per):
            o_ref[c * per + s] = y[:, s * LANES:(s + 1) * LANES]


def _norm_in_proj(x, gain, w):
    T, K = x.shape
    N = w.shape[1]
    assert N % MM_NCHUNK == 0
    return pl.pallas_call(
        _mm_in_kernel, grid=(T // MM_TOK,),
        in_specs=[pl.BlockSpec((MM_TOK, K), lambda i: (i, 0)),
                  pl.BlockSpec((1, K), lambda i: (0, 0)),
                  pl.BlockSpec((K, N), lambda i: (0, 0))],
        out_specs=pl.BlockSpec((N // LANES, MM_TOK, LANES), lambda i: (0, i, 0)),
        out_shape=jax.ShapeDtypeStruct((N // LANES, T, LANES), F32),
        compiler_params=pltpu.CompilerParams(dimension_semantics=("arbitrary",),
                                             vmem_limit_bytes=VMEM_LIMIT),
        name="norm_in_proj")(x, gain.astype(F32).reshape(1, K), w.astype(BF16))


def _mm_out_kernel(a_ref, b_ref, w_ref, x_ref, g_ref, xo_ref, xn_ref):
    lhs = jnp.concatenate([a_ref[s].astype(BF16) for s in range(a_ref.shape[0])] +
                          [b_ref[s].astype(BF16) for s in range(b_ref.shape[0])], axis=-1)
    xo = x_ref[...] + jnp.dot(lhs, w_ref[...], preferred_element_type=F32)
    xo_ref[...] = xo
    xn_ref[...] = _rms_rows(xo, g_ref[...])


def _out_proj_residual(a_slabs, b_slabs, w, x, gain):
    NA_, T, _ = a_slabs.shape
    NB_ = b_slabs.shape[0]
    N = w.shape[1]
    row = pl.BlockSpec((MM_TOK, N), lambda i: (i, 0))
    return pl.pallas_call(
        _mm_out_kernel, grid=(T // MM_TOK,),
        in_specs=[pl.BlockSpec((NA_, MM_TOK, LANES), lambda i: (0, i, 0)),
                  pl.BlockSpec((NB_, MM_TOK, LANES), lambda i: (0, i, 0)),
                  pl.BlockSpec(((NA_ + NB_) * LANES, N), lambda i: (0, 0)),
                  row,
                  pl.BlockSpec((1, N), lambda i: (0, 0))],
        out_specs=[row, row],
        out_shape=[jax.ShapeDtypeStruct((T, N), F32), jax.ShapeDtypeStruct((T, N), F32)],
        compiler_params=pltpu.CompilerParams(dimension_semantics=("arbitrary",),
                                             vmem_limit_bytes=VMEM_LIMIT),
        name="out_proj_residual")(a_slabs, b_slabs, w.astype(BF16), x, gain.astype(F32).reshape(1, N))


def _dft_tables(L):
    N = 2 * L
    NF = -(-(L + 1) // LANES) * LANES
    k = jnp.arange(NF, dtype=jnp.int32)[:, None]
    n = jnp.arange(L, dtype=jnp.int32)[None, :]
    ang = ((k * n) % N).astype(F32) * (2.0 * math.pi / N)
    live = k <= L
    fr = jnp.where(live, jnp.cos(ang), 0.0).astype(BF16)
    fi = jnp.where(live, -jnp.sin(ang), 0.0).astype(BF16)
    kk = np.arange(NF)
    wk = np.where((kk == 0) | (kk == L), 1.0, np.where(kk < L, 2.0, 0.0)) / N
    return fr, fi, jnp.asarray(wk.astype(np.float32).reshape(NF, 1))


def _filter_consts(L):
    t_norm = jnp.linspace(0.0, 1.0, L, dtype=F32)[:, None]
    w = 2.0 * math.pi * jnp.arange(L, dtype=F32)[:, None] / L
    f = jnp.linspace(1e-4, HY_BANDS - 1, HY_BANDS, dtype=F32)[None, :]
    emb = jnp.concatenate([t_norm, jnp.cos(f * w), -jnp.sin(f * w)], axis=-1)
    emb = jnp.pad(emb, ((0, 0), (0, HY_EMB_PAD - HY_EMB)))
    deltas = jnp.abs(jnp.linspace(HY_MIN_DECAY, HY_MAX_DECAY, HY_CHANNELS, dtype=F32))
    window = jnp.exp(-t_norm * deltas[None, :]) + HY_MOD_SHIFT
    return emb, window


def _filt_kernel(emb_ref, w1_ref, b1_ref, w2_ref, b2_ref, w3_ref, b3_ref, wo_ref, freq_ref, win_ref,
                 dr_ref, di_ref, hr_ref, hi_ref):
    fr = freq_ref[...]
    h = jnp.sin(fr * (_dot3(emb_ref[...], w1_ref[...]) + b1_ref[...]))
    h = jnp.sin(fr * (_dot3(h, w2_ref[...]) + b2_ref[...]))
    h = jnp.sin(fr * (_dot3(h, w3_ref[...]) + b3_ref[...]))
    win = win_ref[...]
    fwd = _dot3(h, wo_ref[0, 0]) * win
    bwd = _dot3(h, wo_ref[0, 1]) * win
    row = lax.broadcasted_iota(jnp.int32, bwd.shape, 0)
    bwd = jnp.where(row == 0, 0.0, bwd)
    sh, sl = _split(fwd + bwd)
    dh, dl = _split(fwd - bwd)
    d = lambda x, y: jnp.dot(x, y, preferred_element_type=F32)
    hr_ref[0] = d(dr_ref[...], sh) + d(dr_ref[...], sl)
    hi_ref[0] = d(di_ref[...], dh) + d(di_ref[...], dl)


def _hyena_filter_spectrum(fw1, fb1, fw2, fb2, fw3, fb3, fw_out, freq, emb, window, dr, di):
    L = emb.shape[0]
    NF = dr.shape[0]
    C = HY_CHANNELS
    w1 = jnp.pad(fw1.astype(F32), ((0, HY_EMB_PAD - HY_EMB), (0, 0)))
    wo = fw_out.astype(F32).reshape(HY_HID, HY_ORDER, 2, C).transpose(1, 2, 0, 3)
    vec = lambda v: v.astype(F32).reshape(1, HY_HID)
    full = lambda a: pl.BlockSpec(a.shape, lambda o, c: (0,) * a.ndim)
    args = (emb, w1, vec(fb1), fw2.astype(F32), vec(fb2), fw3.astype(F32), vec(fb3))
    once = pl.Buffered(1)
    out = pl.BlockSpec((1, NF, HY_FILT_CB), lambda o, c: (o, 0, c))
    return pl.pallas_call(
        _filt_kernel, grid=(HY_ORDER, C // HY_FILT_CB),
        in_specs=[full(a) for a in args] + [
            pl.BlockSpec((1, 2, HY_HID, HY_FILT_CB), lambda o, c: (o, 0, 0, c)),
            pl.BlockSpec((1, HY_HID), lambda o, c: (0, 0)),
            pl.BlockSpec((L, HY_FILT_CB), lambda o, c: (0, c)),
            pl.BlockSpec((NF, L), lambda o, c: (0, 0), pipeline_mode=once),
            pl.BlockSpec((NF, L), lambda o, c: (0, 0), pipeline_mode=once)],
        out_specs=[out, out],
        out_shape=[jax.ShapeDtypeStruct((HY_ORDER, NF, C), F32)] * 2,
        compiler_params=pltpu.CompilerParams(dimension_semantics=("arbitrary", "arbitrary"),
                                             vmem_limit_bytes=VMEM_LIMIT_HYENA),
        name="hyena_filter")(*args, wo, vec(freq), window, dr, di)


def _conv_kernel(v_ref, x1_ref, x2_ref, cw_ref, cb_ref, skip_ref, hr_ref, hi_ref, dr_ref, di_ref, wk_ref, o_ref):
    S = v_ref.shape[2]
    nsl = v_ref.shape[0]
    row = lax.broadcasted_iota(jnp.int32, (S, LANES), 0)

    def short(u_ref, g):
        cols = []
        for s in range(nsl):
            u = u_ref[s, 0]
            w = cw_ref[g, s]
            up = jnp.where(row == 0, 0.0, pltpu.roll(u, 1, 0))
            un = jnp.where(row == S - 1, 0.0, pltpu.roll(u, S - 1, 0))
            cols.append(up * w[0:1] + u * w[1:2] + un * w[2:3] + cb_ref[g, s])
        return jnp.concatenate(cols, axis=-1)

    d = lambda x, y: jnp.dot(x, y, preferred_element_type=F32)
    wk = wk_ref[...]

    def long_conv(z, o):
        zb = z.astype(BF16)
        zr = d(dr_ref[...], zb)
        zi = d(di_ref[...], zb)
        hr = hr_ref[o]
        hi = hi_ref[o]
        yr = ((zr * hr - zi * hi) * wk).T.astype(BF16)
        yi = ((zr * hi + zi * hr) * wk).T.astype(BF16)
        y = d(yr, dr_ref[...]) + d(yi, di_ref[...])
        return y.T + z * skip_ref[o]

    v = short(v_ref, 0)
    z = short(x1_ref, 1) * long_conv(v, 0)
    out = short(x2_ref, 2) * long_conv(z, 1)
    for s in range(nsl):
        o_ref[s, 0] = out[:, s * LANES:(s + 1) * LANES]


def _hyena_conv(slabs, conv_w, conv_b, skip, hr, hi, dr, di, wk, B, S):
    C = HY_CHANNELS
    NF = dr.shape[0]
    nsl = HY_CB // LANES
    nc = C // HY_CB
    x = slabs.reshape(slabs.shape[0], B, S, LANES)
    cw = conv_w.astype(F32).reshape(3, 3, C // LANES, LANES).transpose(1, 2, 0, 3)
    cb = conv_b.astype(F32).reshape(3, C // LANES, 1, LANES)
    sk = skip.astype(F32).reshape(HY_ORDER, 1, C)
    spec = lambda g: pl.BlockSpec((nsl, 1, S, LANES), lambda c, b, g=g: (g * nc + c, b, 0, 0))
    once = pl.Buffered(1)
    return pl.pallas_call(
        _conv_kernel, grid=(nc, B),
        in_specs=[spec(0), spec(1), spec(2),
                  pl.BlockSpec((3, nsl, 3, LANES), lambda c, b: (0, c, 0, 0)),
                  pl.BlockSpec((3, nsl, 1, LANES), lambda c, b: (0, c, 0, 0)),
                  pl.BlockSpec((HY_ORDER, 1, HY_CB), lambda c, b: (0, 0, c)),
                  pl.BlockSpec((HY_ORDER, NF, HY_CB), lambda c, b: (0, 0, c), pipeline_mode=once),
                  pl.BlockSpec((HY_ORDER, NF, HY_CB), lambda c, b: (0, 0, c), pipeline_mode=once),
                  pl.BlockSpec((NF, S), lambda c, b: (0, 0), pipeline_mode=once),
                  pl.BlockSpec((NF, S), lambda c, b: (0, 0), pipeline_mode=once),
                  pl.BlockSpec((NF, 1), lambda c, b: (0, 0))],
        out_specs=pl.BlockSpec((nsl, 1, S, LANES), lambda c, b: (c, b, 0, 0)),
        out_shape=jax.ShapeDtypeStruct((C // LANES, B, S, LANES), F32),
        compiler_params=pltpu.CompilerParams(dimension_semantics=("arbitrary", "arbitrary"),
                                             vmem_limit_bytes=VMEM_LIMIT_HYENA),
        name="hyena_conv")(x, x, x, cw, cb, sk, hr, hi, dr, di, wk)


def _ret_kernel(lg_ref, q_ref, k_ref, v_ref, g_ref, cos_ref, sin_ref, gn_ref, o_ref, qr_ref, kr_ref, vb_ref):
    hp = pl.program_id(0)
    S = q_ref.shape[1]
    cos = cos_ref[...]
    sin = sin_ref[...]
    qr_ref[...] = _rope2(q_ref[0], cos, sin).astype(BF16)
    kr_ref[...] = (_rope2(k_ref[0], cos, sin) * HEAD_DIM ** -0.5).astype(BF16)
    vb_ref[...] = v_ref[0].astype(BF16)
    first = lax.broadcasted_iota(jnp.int32, (RET_BQ, LANES), 1) < HEAD_DIM
    m_idx = lax.broadcasted_iota(jnp.int32, (RET_BQ, S), 1)
    n_loc = lax.broadcasted_iota(jnp.int32, (RET_BQ, S), 0)

    def body(r, c):
        row0 = pl.multiple_of(r * RET_BQ, RET_BQ)
        diff = (n_loc + row0 - m_idx).astype(F32)
        q2 = qr_ref[pl.ds(row0, RET_BQ), :]
        ys = []
        for hh in range(2):
            lgf = lg_ref[0, 2 * hp + hh]
            lgb = lg_ref[1, 2 * hp + hh]
            dec = jnp.exp(diff * jnp.where(diff >= 0.0, lgf, -lgb))
            qm = jnp.where(first if hh == 0 else ~first, q2, jnp.zeros_like(q2))
            s = lax.dot_general(qm, kr_ref[...], (((1,), (1,)), ((), ())), preferred_element_type=F32)
            ys.append(jnp.dot((s * dec).astype(BF16), vb_ref[...], preferred_element_type=F32))
        y = jnp.where(first, ys[0], ys[1])
        y = y * lax.rsqrt(_pair_sums(y * y) * (1.0 / HEAD_DIM) + EPS) * gn_ref[0]
        g = g_ref[0, pl.ds(row0, RET_BQ), :]
        o_ref[0, pl.ds(row0, RET_BQ), :] = g * jax.nn.sigmoid(g) * y
        return c

    lax.fori_loop(0, S // RET_BQ, body, 0)


def _retention(slabs, first_slab, log_decay_param, gn_gain, cos, sin, B, S):
    HP = HEAD_PAIRS
    x = slabs.reshape(slabs.shape[0], B, S, LANES)
    lg = -jnp.exp(log_decay_param.astype(F32))
    gn = gn_gain.astype(F32).reshape(HP, 1, LANES)
    spec = lambda g: pl.BlockSpec((None, 1, S, LANES), lambda h, b, g=g: (first_slab + g * HP + h, b, 0, 0))
    tab = pl.BlockSpec((S, LANES), lambda h, b: (0, 0))
    return pl.pallas_call(
        _ret_kernel, grid=(HP, B),
        in_specs=[pl.BlockSpec(memory_space=pltpu.SMEM), spec(0), spec(1), spec(2), spec(3), tab, tab,
                  pl.BlockSpec((None, 1, LANES), lambda h, b: (h, 0, 0))],
        out_specs=pl.BlockSpec((None, 1, S, LANES), lambda h, b: (h, b, 0, 0)),
        out_shape=jax.ShapeDtypeStruct((HP, B, S, LANES), F32),
        scratch_shapes=[pltpu.VMEM((S, LANES), BF16)] * 3,
        compiler_params=pltpu.CompilerParams(dimension_semantics=("arbitrary", "arbitrary"),
                                             vmem_limit_bytes=VMEM_LIMIT),
        name="retention")(lg, x, x, x, x, cos, sin, gn)


def _na_kernel(q_ref, k_ref, v_ref, gq_ref, gk_ref, bias_ref, o_ref, qn_ref, kn_ref, *, rows):
    scale = HEAD_DIM ** -0.5
    qn_ref[...] = (_head_rms2(q_ref[0], gq_ref[...]) * scale).astype(BF16)
    kn_ref[...] = _head_rms2(k_ref[0], gk_ref[...]).astype(BF16)
    lane = lax.broadcasted_iota(jnp.int32, (GRID_W, LANES), 1)
    first = lane < HEAD_DIM
    half_rows = NA_ROWS // 2

    def body(r, c):
        r0 = jnp.clip(r - half_rows, 0, rows - NA_ROWS)
        pat = jnp.where(r < half_rows, r,
                        jnp.where(r > rows - half_rows, r - (rows - NA_ROWS), half_rows))
        q2 = qn_ref[pl.ds(pl.multiple_of(r * GRID_W, GRID_W), GRID_W), :]
        kstart = pl.multiple_of(r0 * GRID_W, GRID_W)
        k2 = kn_ref[pl.ds(kstart, NA_ROWS * GRID_W), :]
        v2 = v_ref[0, pl.ds(kstart, NA_ROWS * GRID_W), :].astype(BF16)
        outs = []
        for hh in range(2):
            qm = jnp.where(first if hh == 0 else ~first, q2, jnp.zeros_like(q2))
            s = lax.dot_general(qm, k2, (((1,), (1,)), ((), ())), preferred_element_type=F32)
            s = s + bias_ref[hh, pat]
            m = jnp.max(s, axis=-1, keepdims=True)
            e = jnp.exp(s - m)
            l = jnp.sum(e, axis=-1, keepdims=True)
            outs.append(jnp.dot(e.astype(BF16), v2, preferred_element_type=F32) / l)
        o_ref[0, pl.ds(pl.multiple_of(r * GRID_W, GRID_W), GRID_W), :] = jnp.where(first, outs[0], outs[1])
        return c

    lax.fori_loop(0, rows, body, 0)


def _na_bias_table(rpb, rows):
    kh = NA_ROWS
    c = np.arange(GRID_W)
    c0 = np.clip(c - NA_COLS // 2, 0, GRID_W - NA_COLS)
    col_valid = (c[None, :] >= c0[:, None]) & (c[None, :] < c0[:, None] + NA_COLS)
    col_off = np.clip(c[None, :] - c[:, None] + NA_COLS - 1, 0, 2 * NA_COLS - 2)
    pat_rows = list(range(kh // 2)) + [kh // 2] + list(range(rows - kh // 2 + 1, rows))
    n_off = 2 * NA_COLS - 1
    idx = np.zeros((len(pat_rows), GRID_W, kh, GRID_W), np.int32)
    for p, r in enumerate(pat_rows):
        r0 = min(max(r - kh // 2, 0), rows - kh)
        ro = r0 + np.arange(kh) - r + NA_ROWS - 1
        idx[p] = ro[None, :, None] * n_off + col_off[:, None, :]
    H = rpb.shape[0]
    b = jnp.take(rpb.astype(F32).reshape(H, -1), idx.reshape(-1), axis=1).reshape((H,) + idx.shape)
    b = jnp.where(col_valid[None, None, :, None, :], b, NEG)
    return b.reshape(H, len(pat_rows), GRID_W, kh * GRID_W)


def _neighbourhood_attention(slabs, gq, gk, rpb, B, S):
    HP = HEAD_PAIRS
    rows = S // GRID_W
    assert rows >= NA_ROWS
    n_pat = NA_ROWS
    bias = _na_bias_table(rpb, rows).reshape(HP, 2, n_pat, GRID_W, NA_ROWS * GRID_W)
    g2q = jnp.tile(gq.astype(F32), 2).reshape(1, LANES)
    g2k = jnp.tile(gk.astype(F32), 2).reshape(1, LANES)
    x = slabs.reshape(slabs.shape[0], B, S, LANES)
    spec = lambda g: pl.BlockSpec((None, 1, S, LANES), lambda h, b, g=g: (g * HP + h, b, 0, 0))
    vec = pl.BlockSpec((1, LANES), lambda h, b: (0, 0))
    return pl.pallas_call(
        functools.partial(_na_kernel, rows=rows), grid=(HP, B),
        in_specs=[spec(0), spec(1), spec(2), vec, vec,
                  pl.BlockSpec((None, 2, n_pat, GRID_W, NA_ROWS * GRID_W), lambda h, b: (h, 0, 0, 0, 0))],
        out_specs=pl.BlockSpec((None, 1, S, LANES), lambda h, b: (h, b, 0, 0)),
        out_shape=jax.ShapeDtypeStruct((HP, B, S, LANES), F32),
        scratch_shapes=[pltpu.VMEM((S, LANES), BF16), pltpu.VMEM((S, LANES), BF16)],
        compiler_params=pltpu.CompilerParams(dimension_semantics=("arbitrary", "arbitrary"),
                                             vmem_limit_bytes=VMEM_LIMIT),
        name="nbr_attention")(x, x, x, g2q, g2k, bias)


def _dil_kernel(q_ref, k_ref, v_ref, gq_ref, gk_ref, cos_ref, sin_ref, o_ref,
                qn_ref, kn_ref, qd_ref, kd_ref, vd_ref, od_ref, ld_ref, os_ref, ls_ref):
    S = q_ref.shape[1]
    BLK = DIL_BLK
    BQ = DIL_BQ
    NK = BQ + 2 * BLK
    cos = cos_ref[...]
    sin = sin_ref[...]
    qn_ref[...] = _rope2(_head_rms2(q_ref[0], gq_ref[...]), cos, sin) * HEAD_DIM ** -0.5
    kn_ref[...] = _rope2(_head_rms2(k_ref[0], gk_ref[...]), cos, sin)
    first = lax.broadcasted_iota(jnp.int32, (BQ, LANES), 1) < HEAD_DIM
    qi = lax.broadcasted_iota(jnp.int32, (BQ, NK), 0)
    kc = lax.broadcasted_iota(jnp.int32, (BQ, NK), 1) - BLK
    band = jnp.abs(kc - qi) <= BLK
    for bi, (window, dil) in enumerate(DIL_BRANCHES):
        assert window // (2 * dil) == BLK
        Ls = S // dil
        shift = Ls.bit_length() - 1
        assert 1 << shift == Ls and Ls >= 2 * BLK
        for r in range(dil):
            sl = pl.ds(r, Ls, stride=dil) if dil > 1 else pl.ds(0, Ls)
            qd_ref[pl.ds(r * Ls, Ls), :] = qn_ref[sl, :].astype(BF16)
            kd_ref[pl.ds(r * Ls, Ls), :] = kn_ref[sl, :].astype(BF16)
            vd_ref[pl.ds(r * Ls, Ls), :] = v_ref.at[0][sl, :].astype(BF16)

        def body(fb, c):
            row0 = pl.multiple_of(fb * BQ, BQ)
            prev0 = pl.multiple_of(jnp.maximum(row0 - BLK, 0), BLK)
            next0 = pl.multiple_of(jnp.minimum(row0 + BQ, S - BLK), BLK)
            kk = jnp.concatenate([kd_ref[pl.ds(prev0, BLK), :], kd_ref[pl.ds(row0, BQ), :],
                                  kd_ref[pl.ds(next0, BLK), :]], axis=0)
            vv = jnp.concatenate([vd_ref[pl.ds(prev0, BLK), :], vd_ref[pl.ds(row0, BQ), :],
                                  vd_ref[pl.ds(next0, BLK), :]], axis=0)
            valid = band & (((row0 + kc) >> shift) == ((row0 + qi) >> shift))
            q2 = qd_ref[pl.ds(row0, BQ), :]
            outs, lses = [], []
            for hh in range(2):
                qm = jnp.where(first if hh == 0 else ~first, q2, jnp.zeros_like(q2))
                s = lax.dot_general(qm, kk, (((1,), (1,)), ((), ())), preferred_element_type=F32)
                s = jnp.where(valid, s, NEG)
                m = jnp.max(s, axis=-1, keepdims=True)
                p = jnp.exp(s - m)
                den = jnp.sum(p, axis=-1, keepdims=True)
                outs.append(jnp.dot(p.astype(BF16), vv, preferred_element_type=F32) / den)
                lses.append(m + jnp.log(den))
            od_ref[pl.ds(row0, BQ), :] = jnp.where(first, outs[0], outs[1])
            ld_ref[pl.ds(row0, BQ), :] = jnp.where(first, lses[0], lses[1])
            return c

        lax.fori_loop(0, S // BQ, body, 0)
        for r in range(dil):
            sl = pl.ds(r, Ls, stride=dil) if dil > 1 else pl.ds(0, Ls)
            os_ref.at[bi][sl, :] = od_ref[pl.ds(r * Ls, Ls), :]
            ls_ref.at[bi][sl, :] = ld_ref[pl.ds(r * Ls, Ls), :]
    l0, l1, l2 = ls_ref[0], ls_ref[1], ls_ref[2]
    mx = jnp.maximum(jnp.maximum(l0, l1), l2)
    w0, w1, w2 = jnp.exp(l0 - mx), jnp.exp(l1 - mx), jnp.exp(l2 - mx)
    o_ref[0] = (w0 * os_ref[0] + w1 * os_ref[1] + w2 * os_ref[2]) / (w0 + w1 + w2)


def _dilated_attention(slabs, first_slab, gq, gk, cos, sin, B, S):
    HP = HEAD_PAIRS
    x = slabs.reshape(slabs.shape[0], B, S, LANES)
    g2q = jnp.tile(gq.astype(F32), 2).reshape(1, LANES)
    g2k = jnp.tile(gk.astype(F32), 2).reshape(1, LANES)
    spec = lambda g: pl.BlockSpec((None, 1, S, LANES), lambda h, b, g=g: (first_slab + g * HP + h, b, 0, 0))
    tab = pl.BlockSpec((S, LANES), lambda h, b: (0, 0))
    vec = pl.BlockSpec((1, LANES), lambda h, b: (0, 0))
    return pl.pallas_call(
        _dil_kernel, grid=(HP, B),
        in_specs=[spec(0), spec(1), spec(2), vec, vec, tab, tab],
        out_specs=pl.BlockSpec((None, 1, S, LANES), lambda h, b: (h, b, 0, 0)),
        out_shape=jax.ShapeDtypeStruct((HP, B, S, LANES), F32),
        scratch_shapes=[pltpu.VMEM((S, LANES), F32), pltpu.VMEM((S, LANES), F32),
                        pltpu.VMEM((S, LANES), BF16), pltpu.VMEM((S, LANES), BF16),
                        pltpu.VMEM((S, LANES), BF16),
                        pltpu.VMEM((S, LANES), F32), pltpu.VMEM((S, LANES), F32),
                        pltpu.VMEM((len(DIL_BRANCHES), S, LANES), F32),
                        pltpu.VMEM((len(DIL_BRANCHES), S, LANES), F32)],
        compiler_params=pltpu.CompilerParams(dimension_semantics=("arbitrary", "arbitrary"),
                                             vmem_limit_bytes=VMEM_LIMIT),
        name="dilated_attention")(x, x, x, g2q, g2k, cos, sin)


def _stair_tables():
    a_idx = [0] * 16 + [a for a in range(1, 8) for _ in range(8)] + list(range(8, 16))
    b_idx = list(range(16)) + [b for _ in range(1, 8) for b in range(8)] + [0] * 8
    a_idx = np.array(a_idx)
    b_idx = np.array(b_idx)
    valid = (a_idx + 1) * (b_idx + 1) <= PEER_TOPK
    return a_idx, b_idx, valid


_ST_A, _ST_B, _ST_VALID = _stair_tables()
N_CAND = len(_ST_A)


def _top16_rows(sc, iota):
    vals, idxs = [], []
    big = float(sc.shape[0])
    for r in range(PEER_TOPK):
        m = jnp.max(sc, axis=0, keepdims=True)
        idx = jnp.min(jnp.where(sc == m, iota, big), axis=0, keepdims=True)
        vals.append(m)
        idxs.append(idx)
        if r + 1 < PEER_TOPK:
            sc = jnp.where(iota == idx, NEG_INF, sc)
    return jnp.concatenate(vals, axis=0), jnp.concatenate(idxs, axis=0)


def _route_kernel(xn_ref, wqT_ref, keys_ref, flat_ref, e_ref, g_ref, qT_ref, eT_ref, gT_ref):
    Tb = xn_ref.shape[0]
    xb = xn_ref[...].astype(BF16)
    qT_ref[...] = lax.dot_general(wqT_ref[...], xb, (((1,), (1,)), ((), ())), preferred_element_type=F32)
    iota = lax.broadcasted_iota(jnp.int32, (PEER_KEYS, Tb), 0).astype(F32)
    flat = jnp.broadcast_to(flat_ref[...], (N_CAND, Tb))
    for h in range(PEER_HEADS):
        s, ix = [], []
        for p in range(2):
            r0 = (h * 2 + p) * (PEER_DK // 2)
            sc = jnp.dot(keys_ref[h * 2 + p], qT_ref[pl.ds(r0, PEER_DK // 2), :].astype(BF16),
                         preferred_element_type=F32)
            sp, ip = _top16_rows(sc, iota)
            s.append(sp)
            ix.append(ip)
        s0, s1 = s
        i0, i1 = ix[0] * float(PEER_KEYS), ix[1]
        parts_s = [s0[0:1] + s1]
        parts_e = [i0[0:1] + i1]
        for a in range(1, 8):
            parts_s.append(s0[a:a + 1] + s1[0:8])
            parts_e.append(i0[a:a + 1] + i1[0:8])
        parts_s.append(s0[8:16] + s1[0:1])
        parts_e.append(i0[8:16] + i1[0:1])
        cand = jnp.where(flat < 256.0, jnp.concatenate(parts_s, axis=0), NEG_INF)
        eid = jnp.concatenate(parts_e, axis=0)
        gs, es = [], []
        for r in range(PEER_TOPK):
            m = jnp.max(cand, axis=0, keepdims=True)
            fsel = jnp.min(jnp.where(cand == m, flat, float(1 << 20)), axis=0, keepdims=True)
            hit = flat == fsel
            es.append(jnp.max(jnp.where(hit, eid, -1.0), axis=0, keepdims=True))
            gs.append(m)
            if r + 1 < PEER_TOPK:
                cand = jnp.where(hit, NEG_INF, cand)
        g = jnp.concatenate(gs, axis=0)
        ex = jnp.exp(g - g[0:1])
        eT_ref[pl.ds(h * PEER_TOPK, PEER_TOPK), :] = jnp.concatenate(es, axis=0).astype(jnp.int32)
        gT_ref[pl.ds(h * PEER_TOPK, PEER_TOPK), :] = ex / jnp.sum(ex, axis=0, keepdims=True)
    e_ref[...] = eT_ref[...].T
    g_ref[...] = gT_ref[...].T


def _peer_route(xn, w_q, sub_keys):
    T = xn.shape[0]
    Tb = ROUTE_TOK
    wqT = w_q.T.astype(BF16)
    keys = sub_keys.reshape(PEER_HEADS * 2, PEER_KEYS, PEER_DK // 2).astype(BF16)
    flat = np.where(_ST_VALID, _ST_A * 16 + _ST_B, 1 << 16).astype(np.float32).reshape(N_CAND, 1)
    return pl.pallas_call(
        _route_kernel, grid=(T // Tb,),
        in_specs=[pl.BlockSpec((Tb, D_MODEL), lambda i: (i, 0)),
                  pl.BlockSpec((D_MODEL, D_MODEL), lambda i: (0, 0)),
                  pl.BlockSpec((PEER_HEADS * 2, PEER_KEYS, PEER_DK // 2), lambda i: (0, 0, 0)),
                  pl.BlockSpec((N_CAND, 1), lambda i: (0, 0))],
        out_specs=[pl.BlockSpec((Tb, PEER_SEL), lambda i: (i, 0)),
                   pl.BlockSpec((Tb, PEER_SEL), lambda i: (i, 0))],
        out_shape=[jax.ShapeDtypeStruct((T, PEER_SEL), jnp.int32),
                   jax.ShapeDtypeStruct((T, PEER_SEL), F32)],
        scratch_shapes=[pltpu.VMEM((D_MODEL, Tb), F32), pltpu.VMEM((PEER_SEL, Tb), jnp.int32),
                        pltpu.VMEM((PEER_SEL, Tb), F32)],
        compiler_params=pltpu.CompilerParams(dimension_semantics=("arbitrary",),
                                             vmem_limit_bytes=VMEM_LIMIT),
        name="peer_route")(xn, wqT, keys, jnp.asarray(flat))


def _peer_eval_kernel(idx_cur, idx_nxt, x_ref, r_ref, g_ref, uv_hbm, o_ref, buf, sem):
    i = pl.program_id(0)
    n = pl.num_programs(0)
    G = PEER_TOK

    def issue_tok(idx_ref, row, slot, g):
        for k in range(PEER_SEL):
            e = idx_ref[row, k]
            pltpu.make_async_copy(uv_hbm.at[e], buf.at[slot, g, k // SUBLANES, :, k % SUBLANES, :],
                                  sem.at[slot]).start(priority=k % DMA_THREADS)

    def wait_slot(slot):
        pltpu.make_async_copy(buf.at[slot], buf.at[slot], sem.at[slot]).wait()

    def compute_tok(slot, g, row, gates_t):
        xg = x_ref[row]
        acc = buf[slot, g, :, 0].reshape(PEER_SEL, LANES) * xg[0:1, :]
        for c in range(1, SUBLANES):
            acc = acc + buf[slot, g, :, c].reshape(PEER_SEL, LANES) * xg[c:c + 1, :]
        h = jnp.sum(acc, axis=-1, keepdims=True)
        a = jax.nn.gelu(h) * gates_t[:, row:row + 1]
        ab = jnp.broadcast_to(a, (PEER_SEL, LANES))
        rows = [jnp.sum(ab * buf[slot, g, :, SUBLANES + c].reshape(PEER_SEL, LANES), axis=0, keepdims=True)
                for c in range(SUBLANES)]
        o_ref[row] = r_ref[row] + jnp.concatenate(rows, axis=0)

    @pl.when(i == 0)
    def _():
        for g in range(G):
            issue_tok(idx_cur, g, 0, g)

    gates_t = g_ref[...].T
    wait_slot(0)
    for g in range(G):
        issue_tok(idx_cur, G + g, 1, g)
        compute_tok(0, g, g, gates_t)
    wait_slot(1)
    for g in range(G):
        issue_tok(idx_nxt, g, 0, g)
        compute_tok(1, g, G + g, gates_t)

    @pl.when(i == n - 1)
    def _():
        wait_slot(0)


def _peer_eval(xt, res, experts, gates, uv):
    T = xt.shape[0]
    G = PEER_TOK
    n = T // (2 * G)
    x3 = xt.reshape(T, SUBLANES, LANES)
    r3 = res.reshape(T, SUBLANES, LANES)
    tok = pl.BlockSpec((2 * G, SUBLANES, LANES), lambda i: (i, 0, 0))
    out = pl.pallas_call(
        _peer_eval_kernel,
        grid=(n,),
        in_specs=[
            pl.BlockSpec((2 * G, PEER_SEL), lambda i: (i, 0), memory_space=pltpu.SMEM),
            pl.BlockSpec((2 * G, PEER_SEL), lambda i: (jnp.minimum(i + 1, n - 1), 0),
                         memory_space=pltpu.SMEM),
            tok,
            tok,
            pl.BlockSpec((2 * G, PEER_SEL), lambda i: (i, 0)),
            pl.BlockSpec(memory_space=pl.ANY),
        ],
        out_specs=tok,
        out_shape=jax.ShapeDtypeStruct((T, SUBLANES, LANES), F32),
        scratch_shapes=[
            pltpu.VMEM((2, G, 2 * SUBLANES, 2 * SUBLANES, SUBLANES, LANES), F32),
            pltpu.SemaphoreType.DMA((2,)),
        ],
        compiler_params=pltpu.CompilerParams(
            dimension_semantics=("arbitrary",),
            vmem_limit_bytes=VMEM_LIMIT),
        name="peer_eval",
    )(experts, experts, x3, r3, gates, uv)
    return out.reshape(T, D_MODEL)


def _peer(xn, res, w_q, sub_keys, u, v):
    E = u.shape[0]
    experts, gates = _peer_route(xn, w_q, sub_keys)
    uv = jnp.concatenate([u.astype(F32).reshape(E, SUBLANES, LANES),
                          v.astype(F32).reshape(E, SUBLANES, LANES)], axis=1)
    return _peer_eval(xn, res, experts, gates, uv)


def kernel(x, mix_norm_gain, ffn_norm_gain, ev_w_in, ev_w_out, hy_conv_w, hy_conv_b, hy_fw1, hy_fb1, hy_fw2, hy_fb2, hy_fw3, hy_fb3, hy_fw_out, hy_freq, hy_skip, ret_log_decay, ret_gn_gain, od_w_in, od_w_out, na_qk_gain, na_rpb, dil_qk_gain, peer_w_q, peer_sub_keys, peer_u, peer_v):
    B, S, D = x.shape
    T = B * S
    depth = mix_norm_gain.shape[0]
    cos, sin = _rope_tables(S)
    emb, window = _filter_consts(S)
    dr, di, wk = _dft_tables(S)
    xt = x.reshape(T, D).astype(F32)
    for layer in range(depth):
        if layer % 2 == 0:
            e = layer // 2
            slabs = _norm_in_proj(xt, mix_norm_gain[layer], ev_w_in[e])
            hr, hi = _hyena_filter_spectrum(hy_fw1[e], hy_fb1[e], hy_fw2[e], hy_fb2[e], hy_fw3[e], hy_fb3[e],
                                            hy_fw_out[e], hy_freq[e], emb, window, dr, di)
            a_out = _hyena_conv(slabs, hy_conv_w[e], hy_conv_b[e], hy_skip[e], hr, hi, dr, di, wk, B, S)
            b_out = _retention(slabs, 3 * HEAD_PAIRS, ret_log_decay[e], ret_gn_gain[e], cos, sin, B, S)
            w_out = ev_w_out[e]
        else:
            o = layer // 2
            slabs = _norm_in_proj(xt, mix_norm_gain[layer], od_w_in[o])
            a_out = _neighbourhood_attention(slabs, na_qk_gain[o, 0], na_qk_gain[o, 1], na_rpb[o], B, S)
            b_out = _dilated_attention(slabs, 3 * HEAD_PAIRS, dil_qk_gain[o, 0], dil_qk_gain[o, 1],
                                       cos, sin, B, S)
            w_out = od_w_out[o]
        xt, xn = _out_proj_residual(a_out.reshape(HEAD_PAIRS, T, LANES), b_out.reshape(HEAD_PAIRS, T, LANES),
                                    w_out, xt, ffn_norm_gain[layer])
        xt = _peer(xn, xt, peer_w_q[layer], peer_sub_keys[layer], peer_u[layer], peer_v[layer])
    return xt.reshape(B, S, D).astype(x.dtype)
```

```python
import functools
import math

import numpy as np
import jax
import jax.numpy as jnp
from jax import lax
from jax.experimental import pallas as pl
from jax.experimental.pallas import tpu as pltpu

F32 = jnp.float32
BF16 = jnp.bfloat16

D_MODEL = 1024
HEAD_DIM = 64
GROUP_WIDTH = D_MODEL // 2
EPS = 1e-6
NEG = -1e30
NEG_INF = float("-inf")
ROPE_THETA = 10000.0

HY_CHANNELS = GROUP_WIDTH
HY_ORDER = 2
HY_EMB = 33
HY_EMB_PAD = 40
HY_BANDS = (HY_EMB - 1) // 2
HY_HID = 64
HY_FAST_DECAY = 0.3
HY_SLOW_DECAY = 1.5
HY_DECAY_TARGET = 1e-2
HY_MAX_DECAY = math.log(HY_DECAY_TARGET) / HY_FAST_DECAY
HY_MIN_DECAY = math.log(HY_DECAY_TARGET) / HY_SLOW_DECAY
HY_MOD_SHIFT = 0.05

GRID_W = 64
NA_HEADS = GROUP_WIDTH // HEAD_DIM
NA_ROWS = 8
NA_COLS = 16

DIL_BRANCHES = ((128, 1), (512, 4), (2048, 16))
DIL_BLK = 64
DIL_BQ = 256

PEER_HEADS = 8
PEER_KEYS = 128
PEER_DK = 128
PEER_TOPK = 16
PEER_SEL = PEER_HEADS * PEER_TOPK

SUBLANES = 8
LANES = 128
assert D_MODEL == SUBLANES * LANES
HEAD_PAIRS = GROUP_WIDTH // LANES
PEER_TOK = 4
PEER_SLOTS = 4
PEER_AHEAD = PEER_SLOTS - 1
ROUTE_TOK = 256
DMA_THREADS = 2
MM_TOK = 256
MM_NCHUNK = 512
RET_BQ = 256
HY_CB = 128
HY_FILT_CB = 256
VMEM_LIMIT = 48 * 1024 * 1024
VMEM_LIMIT_HYENA = 56 * 1024 * 1024


def _split(x):
    hi = x.astype(BF16)
    return hi, (x - hi.astype(F32)).astype(BF16)


def _dot3(a, b):
    ah, al = _split(a)
    bh, bl = _split(b)
    d = lambda x, y: jnp.dot(x, y, preferred_element_type=F32)
    return d(ah, bh) + (d(ah, bl) + d(al, bh))


def _rms_rows(x, gain):
    return x * lax.rsqrt(jnp.mean(x * x, axis=-1, keepdims=True) + EPS) * gain


def _pair_sums(x):
    half = (lax.broadcasted_iota(jnp.int32, (LANES, LANES), 0) // HEAD_DIM ==
            lax.broadcasted_iota(jnp.int32, (LANES, LANES), 1) // HEAD_DIM).astype(BF16)
    hi, lo = _split(x)
    return (jnp.dot(hi, half, preferred_element_type=F32) +
            jnp.dot(lo, half, preferred_element_type=F32))


def _head_rms2(t, gain2):
    ms = _pair_sums(t * t) * (1.0 / HEAD_DIM)
    return t * lax.rsqrt(ms + EPS) * gain2


def _rope_tables(S):
    lane = np.arange(LANES)
    inv = ROPE_THETA ** (-jnp.arange(0, HEAD_DIM, 2, dtype=F32) / HEAD_DIM)
    ang = jnp.arange(S, dtype=F32)[:, None] * inv[None, :]
    cos = jnp.tile(jnp.cos(ang), (1, LANES // (HEAD_DIM // 2)))
    sin = jnp.tile(jnp.sin(ang), (1, LANES // (HEAD_DIM // 2)))
    sign = np.where((lane % HEAD_DIM) < HEAD_DIM // 2, -1.0, 1.0).astype(np.float32)
    return cos, sin * sign[None, :]


def _rope2(t, cos, sin_signed):
    lane = lax.broadcasted_iota(jnp.int32, t.shape, 1)
    lower = (lane % HEAD_DIM) < HEAD_DIM // 2
    partner = jnp.where(lower, pltpu.roll(t, LANES - HEAD_DIM // 2, 1), pltpu.roll(t, HEAD_DIM // 2, 1))
    return t * cos + partner * sin_signed


def _mm_in_kernel(x_ref, g_ref, w_ref, o_ref):
    xn = _rms_rows(x_ref[...], g_ref[...]).astype(BF16)
    per = MM_NCHUNK // LANES
    for c in range(o_ref.shape[0] // per):
        y = jnp.dot(xn, w_ref[:, c * MM_NCHUNK:(c + 1) * MM_NCHUNK], preferred_element_type=F32)
        for s in range(per):
            o_ref[c * per + s] = y[:, s * LANES:(s + 1) * LANES]


def _norm_in_proj(x, gain, w):
    T, K = x.shape
    N = w.shape[1]
    assert N % MM_NCHUNK == 0
    return pl.pallas_call(
        _mm_in_kernel, grid=(T // MM_TOK,),
        in_specs=[pl.BlockSpec((MM_TOK, K), lambda i: (i, 0)),
                  pl.BlockSpec((1, K), lambda i: (0, 0)),
                  pl.BlockSpec((K, N), lambda i: (0, 0))],
        out_specs=pl.BlockSpec((N // LANES, MM_TOK, LANES), lambda i: (0, i, 0)),
        out_shape=jax.ShapeDtypeStruct((N // LANES, T, LANES), F32),
        compiler_params=pltpu.CompilerParams(dimension_semantics=("arbitrary",),
                                             vmem_limit_bytes=VMEM_LIMIT),
        name="norm_in_proj")(x, gain.astype(F32).reshape(1, K), w.astype(BF16))


def _mm_out_kernel(a_ref, b_ref, w_ref, x_ref, g_ref, xo_ref, xn_ref):
    lhs = jnp.concatenate([a_ref[s].astype(BF16) for s in range(a_ref.shape[0])] +
                          [b_ref[s].astype(BF16) for s in range(b_ref.shape[0])], axis=-1)
    xo = x_ref[...] + jnp.dot(lhs, w_ref[...], preferred_element_type=F32)
    xo_ref[...] = xo
    xn_ref[...] = _rms_rows(xo, g_ref[...])


def _out_proj_residual(a_slabs, b_slabs, w, x, gain):
    NA_, T, _ = a_slabs.shape
    NB_ = b_slabs.shape[0]
    N = w.shape[1]
    row = pl.BlockSpec((MM_TOK, N), lambda i: (i, 0))
    return pl.pallas_call(
        _mm_out_kernel, grid=(T // MM_TOK,),
        in_specs=[pl.BlockSpec((NA_, MM_TOK, LANES), lambda i: (0, i, 0)),
                  pl.BlockSpec((NB_, MM_TOK, LANES), lambda i: (0, i, 0)),
                  pl.BlockSpec(((NA_ + NB_) * LANES, N), lambda i: (0, 0)),
                  row,
                  pl.BlockSpec((1, N), lambda i: (0, 0))],
        out_specs=[row, row],
        out_shape=[jax.ShapeDtypeStruct((T, N), F32), jax.ShapeDtypeStruct((T, N), F32)],
        compiler_params=pltpu.CompilerParams(dimension_semantics=("arbitrary",),
                                             vmem_limit_bytes=VMEM_LIMIT),
        name="out_proj_residual")(a_slabs, b_slabs, w.astype(BF16), x, gain.astype(F32).reshape(1, N))


def _dft_tables(L):
    N = 2 * L
    NF = -(-(L + 1) // LANES) * LANES
    k = jnp.arange(NF, dtype=jnp.int32)[:, None]
    n = jnp.arange(L, dtype=jnp.int32)[None, :]
    ang = ((k * n) % N).astype(F32) * (2.0 * math.pi / N)
    live = k <= L
    fr = jnp.where(live, jnp.cos(ang), 0.0).astype(BF16)
    fi = jnp.where(live, -jnp.sin(ang), 0.0).astype(BF16)
    kk = np.arange(NF)
    wk = np.where((kk == 0) | (kk == L), 1.0, np.where(kk < L, 2.0, 0.0)) / N
    return fr, fi, jnp.asarray(wk.astype(np.float32).reshape(NF, 1))


def _filter_consts(L):
    t_norm = jnp.linspace(0.0, 1.0, L, dtype=F32)[:, None]
    w = 2.0 * math.pi * jnp.arange(L, dtype=F32)[:, None] / L
    f = jnp.linspace(1e-4, HY_BANDS - 1, HY_BANDS, dtype=F32)[None, :]
    emb = jnp.concatenate([t_norm, jnp.cos(f * w), -jnp.sin(f * w)], axis=-1)
    emb = jnp.pad(emb, ((0, 0), (0, HY_EMB_PAD - HY_EMB)))
    deltas = jnp.abs(jnp.linspace(HY_MIN_DECAY, HY_MAX_DECAY, HY_CHANNELS, dtype=F32))
    window = jnp.exp(-t_norm * deltas[None, :]) + HY_MOD_SHIFT
    return emb, window


def _filt_kernel(emb_ref, w1_ref, b1_ref, w2_ref, b2_ref, w3_ref, b3_ref, wo_ref, freq_ref, win_ref,
                 dr_ref, di_ref, hr_ref, hi_ref):
    fr = freq_ref[...]
    h = jnp.sin(fr * (_dot3(emb_ref[...], w1_ref[...]) + b1_ref[...]))
    h = jnp.sin(fr * (_dot3(h, w2_ref[...]) + b2_ref[...]))
    h = jnp.sin(fr * (_dot3(h, w3_ref[...]) + b3_ref[...]))
    win = win_ref[...]
    fwd = _dot3(h, wo_ref[0, 0]) * win
    bwd = _dot3(h, wo_ref[0, 1]) * win
    row = lax.broadcasted_iota(jnp.int32, bwd.shape, 0)
    bwd = jnp.where(row == 0, 0.0, bwd)
    sh, sl = _split(fwd + bwd)
    dh, dl = _split(fwd - bwd)
    d = lambda x, y: jnp.dot(x, y, preferred_element_type=F32)
    hr_ref[0] = d(dr_ref[...], sh) + d(dr_ref[...], sl)
    hi_ref[0] = d(di_ref[...], dh) + d(di_ref[...], dl)


def _hyena_filter_spectrum(fw1, fb1, fw2, fb2, fw3, fb3, fw_out, freq, emb, window, dr, di):
    L = emb.shape[0]
    NF = dr.shape[0]
    C = HY_CHANNELS
    w1 = jnp.pad(fw1.astype(F32), ((0, HY_EMB_PAD - HY_EMB), (0, 0)))
    wo = fw_out.astype(F32).reshape(HY_HID, HY_ORDER, 2, C).transpose(1, 2, 0, 3)
    vec = lambda v: v.astype(F32).reshape(1, HY_HID)
    full = lambda a: pl.BlockSpec(a.shape, lambda o, c: (0,) * a.ndim)
    args = (emb, w1, vec(fb1), fw2.astype(F32), vec(fb2), fw3.astype(F32), vec(fb3))
    once = pl.Buffered(1)
    out = pl.BlockSpec((1, NF, HY_FILT_CB), lambda o, c: (o, 0, c))
    return pl.pallas_call(
        _filt_kernel, grid=(HY_ORDER, C // HY_FILT_CB),
        in_specs=[full(a) for a in args] + [
            pl.BlockSpec((1, 2, HY_HID, HY_FILT_CB), lambda o, c: (o, 0, 0, c)),
            pl.BlockSpec((1, HY_HID), lambda o, c: (0, 0)),
            pl.BlockSpec((L, HY_FILT_CB), lambda o, c: (0, c)),
            pl.BlockSpec((NF, L), lambda o, c: (0, 0), pipeline_mode=once),
            pl.BlockSpec((NF, L), lambda o, c: (0, 0), pipeline_mode=once)],
        out_specs=[out, out],
        out_shape=[jax.ShapeDtypeStruct((HY_ORDER, NF, C), F32)] * 2,
        compiler_params=pltpu.CompilerParams(dimension_semantics=("arbitrary", "arbitrary"),
                                             vmem_limit_bytes=VMEM_LIMIT_HYENA),
        name="hyena_filter")(*args, wo, vec(freq), window, dr, di)


def _conv_kernel(v_ref, x1_ref, x2_ref, cw_ref, cb_ref, skip_ref, hr_ref, hi_ref, dr_ref, di_ref, wk_ref, o_ref):
    S = v_ref.shape[2]
    nsl = v_ref.shape[0]
    row = lax.broadcasted_iota(jnp.int32, (S, LANES), 0)

    def short(u_ref, g):
        cols = []
        for s in range(nsl):
            u = u_ref[s, 0]
            w = cw_ref[g, s]
            up = jnp.where(row == 0, 0.0, pltpu.roll(u, 1, 0))
            un = jnp.where(row == S - 1, 0.0, pltpu.roll(u, S - 1, 0))
            cols.append(up * w[0:1] + u * w[1:2] + un * w[2:3] + cb_ref[g, s])
        return jnp.concatenate(cols, axis=-1)

    d = lambda x, y: jnp.dot(x, y, preferred_element_type=F32)
    wk = wk_ref[...]

    def long_conv(z, o):
        zb = z.astype(BF16)
        zr = d(dr_ref[...], zb)
        zi = d(di_ref[...], zb)
        hr = hr_ref[o]
        hi = hi_ref[o]
        yr = ((zr * hr - zi * hi) * wk).T.astype(BF16)
        yi = ((zr * hi + zi * hr) * wk).T.astype(BF16)
        y = d(yr, dr_ref[...]) + d(yi, di_ref[...])
        return y.T + z * skip_ref[o]

    v = short(v_ref, 0)
    z = short(x1_ref, 1) * long_conv(v, 0)
    out = short(x2_ref, 2) * long_conv(z, 1)
    for s in range(nsl):
        o_ref[s, 0] = out[:, s * LANES:(s + 1) * LANES]


def _hyena_conv(slabs, conv_w, conv_b, skip, hr, hi, dr, di, wk, B, S):
    C = HY_CHANNELS
    NF = dr.shape[0]
    nsl = HY_CB // LANES
    nc = C // HY_CB
    x = slabs.reshape(slabs.shape[0], B, S, LANES)
    cw = conv_w.astype(F32).reshape(3, 3, C // LANES, LANES).transpose(1, 2, 0, 3)
    cb = conv_b.astype(F32).reshape(3, C // LANES, 1, LANES)
    sk = skip.astype(F32).reshape(HY_ORDER, 1, C)
    spec = lambda g: pl.BlockSpec((nsl, 1, S, LANES), lambda c, b, g=g: (g * nc + c, b, 0, 0))
    once = pl.Buffered(1)
    return pl.pallas_call(
        _conv_kernel, grid=(nc, B),
        in_specs=[spec(0), spec(1), spec(2),
                  pl.BlockSpec((3, nsl, 3, LANES), lambda c, b: (0, c, 0, 0)),
                  pl.BlockSpec((3, nsl, 1, LANES), lambda c, b: (0, c, 0, 0)),
                  pl.BlockSpec((HY_ORDER, 1, HY_CB), lambda c, b: (0, 0, c)),
                  pl.BlockSpec((HY_ORDER, NF, HY_CB), lambda c, b: (0, 0, c), pipeline_mode=once),
                  pl.BlockSpec((HY_ORDER, NF, HY_CB), lambda c, b: (0, 0, c), pipeline_mode=once),
                  pl.BlockSpec((NF, S), lambda c, b: (0, 0), pipeline_mode=once),
                  pl.BlockSpec((NF, S), lambda c, b: (0, 0), pipeline_mode=once),
                  pl.BlockSpec((NF, 1), lambda c, b: (0, 0))],
        out_specs=pl.BlockSpec((nsl, 1, S, LANES), lambda c, b: (c, b, 0, 0)),
        out_shape=jax.ShapeDtypeStruct((C // LANES, B, S, LANES), F32),
        compiler_params=pltpu.CompilerParams(dimension_semantics=("arbitrary", "arbitrary"),
                                             vmem_limit_bytes=VMEM_LIMIT_HYENA),
        name="hyena_conv")(x, x, x, cw, cb, sk, hr, hi, dr, di, wk)


def _ret_kernel(lg_ref, q_ref, k_ref, v_ref, g_ref, cos_ref, sin_ref, gn_ref, o_ref, qr_ref, kr_ref, vb_ref):
    hp = pl.program_id(0)
    S = q_ref.shape[1]
    cos = cos_ref[...]
    sin = sin_ref[...]
    qr_ref[...] = _rope2(q_ref[0], cos, sin).astype(BF16)
    kr_ref[...] = (_rope2(k_ref[0], cos, sin) * HEAD_DIM ** -0.5).astype(BF16)
    vb_ref[...] = v_ref[0].astype(BF16)
    first = lax.broadcasted_iota(jnp.int32, (RET_BQ, LANES), 1) < HEAD_DIM
    m_idx = lax.broadcasted_iota(jnp.int32, (RET_BQ, S), 1)
    n_loc = lax.broadcasted_iota(jnp.int32, (RET_BQ, S), 0)

    def body(r, c):
        row0 = pl.multiple_of(r * RET_BQ, RET_BQ)
        diff = (n_loc + row0 - m_idx).astype(F32)
        q2 = qr_ref[pl.ds(row0, RET_BQ), :]
        ys = []
        for hh in range(2):
            lgf = lg_ref[0, 2 * hp + hh]
            lgb = lg_ref[1, 2 * hp + hh]
            dec = jnp.exp(diff * jnp.where(diff >= 0.0, lgf, -lgb))
            qm = jnp.where(first if hh == 0 else ~first, q2, jnp.zeros_like(q2))
            s = lax.dot_general(qm, kr_ref[...], (((1,), (1,)), ((), ())), preferred_element_type=F32)
            ys.append(jnp.dot((s * dec).astype(BF16), vb_ref[...], preferred_element_type=F32))
        y = jnp.where(first, ys[0], ys[1])
        y = y * lax.rsqrt(_pair_sums(y * y) * (1.0 / HEAD_DIM) + EPS) * gn_ref[0]
        g = g_ref[0, pl.ds(row0, RET_BQ), :]
        o_ref[0, pl.ds(row0, RET_BQ), :] = g * jax.nn.sigmoid(g) * y
        return c

    lax.fori_loop(0, S // RET_BQ, body, 0)


def _retention(slabs, first_slab, log_decay_param, gn_gain, cos, sin, B, S):
    HP = HEAD_PAIRS
    x = slabs.reshape(slabs.shape[0], B, S, LANES)
    lg = -jnp.exp(log_decay_param.astype(F32))
    gn = gn_gain.astype(F32).reshape(HP, 1, LANES)
    spec = lambda g: pl.BlockSpec((None, 1, S, LANES), lambda h, b, g=g: (first_slab + g * HP + h, b, 0, 0))
    tab = pl.BlockSpec((S, LANES), lambda h, b: (0, 0))
    return pl.pallas_call(
        _ret_kernel, grid=(HP, B),
        in_specs=[pl.BlockSpec(memory_space=pltpu.SMEM), spec(0), spec(1), spec(2), spec(3), tab, tab,
                  pl.BlockSpec((None, 1, LANES), lambda h, b: (h, 0, 0))],
        out_specs=pl.BlockSpec((None, 1, S, LANES), lambda h, b: (h, b, 0, 0)),
        out_shape=jax.ShapeDtypeStruct((HP, B, S, LANES), F32),
        scratch_shapes=[pltpu.VMEM((S, LANES), BF16)] * 3,
        compiler_params=pltpu.CompilerParams(dimension_semantics=("arbitrary", "arbitrary"),
                                             vmem_limit_bytes=VMEM_LIMIT),
        name="retention")(lg, x, x, x, x, cos, sin, gn)


def _na_kernel(q_ref, k_ref, v_ref, gq_ref, gk_ref, bias_ref, o_ref, qn_ref, kn_ref, *, rows):
    scale = HEAD_DIM ** -0.5
    qn_ref[...] = (_head_rms2(q_ref[0], gq_ref[...]) * scale).astype(BF16)
    kn_ref[...] = _head_rms2(k_ref[0], gk_ref[...]).astype(BF16)
    lane = lax.broadcasted_iota(jnp.int32, (GRID_W, LANES), 1)
    first = lane < HEAD_DIM
    half_rows = NA_ROWS // 2

    def body(r, c):
        r0 = jnp.clip(r - half_rows, 0, rows - NA_ROWS)
        pat = jnp.where(r < half_rows, r,
                        jnp.where(r > rows - half_rows, r - (rows - NA_ROWS), half_rows))
        q2 = qn_ref[pl.ds(pl.multiple_of(r * GRID_W, GRID_W), GRID_W), :]
        kstart = pl.multiple_of(r0 * GRID_W, GRID_W)
        k2 = kn_ref[pl.ds(kstart, NA_ROWS * GRID_W), :]
        v2 = v_ref[0, pl.ds(kstart, NA_ROWS * GRID_W), :].astype(BF16)
        outs = []
        for hh in range(2):
            qm = jnp.where(first if hh == 0 else ~first, q2, jnp.zeros_like(q2))
            s = lax.dot_general(qm, k2, (((1,), (1,)), ((), ())), preferred_element_type=F32)
            s = s + bias_ref[hh, pat]
            m = jnp.max(s, axis=-1, keepdims=True)
            e = jnp.exp(s - m)
            l = jnp.sum(e, axis=-1, keepdims=True)
            outs.append(jnp.dot(e.astype(BF16), v2, preferred_element_type=F32) / l)
        o_ref[0, pl.ds(pl.multiple_of(r * GRID_W, GRID_W), GRID_W), :] = jnp.where(first, outs[0], outs[1])
        return c

    lax.fori_loop(0, rows, body, 0)


def _na_bias_table(rpb, rows):
    kh = NA_ROWS
    c = np.arange(GRID_W)
    c0 = np.clip(c - NA_COLS // 2, 0, GRID_W - NA_COLS)
    col_valid = (c[None, :] >= c0[:, None]) & (c[None, :] < c0[:, None] + NA_COLS)
    col_off = np.clip(c[None, :] - c[:, None] + NA_COLS - 1, 0, 2 * NA_COLS - 2)
    pat_rows = list(range(kh // 2)) + [kh // 2] + list(range(rows - kh // 2 + 1, rows))
    row_sel = np.zeros((len(pat_rows), kh, 2 * NA_ROWS - 1), np.float32)
    for p, r in enumerate(pat_rows):
        r0 = min(max(r - kh // 2, 0), rows - kh)
        row_sel[p, np.arange(kh), r0 + np.arange(kh) - r + NA_ROWS - 1] = 1.0
    col_sel = (col_off[None] == np.arange(2 * NA_COLS - 1)[:, None, None]).astype(np.float32)
    b = jnp.einsum('pir,hrc,cqk->hpqik', row_sel, rpb.astype(F32), col_sel, precision=lax.Precision.HIGHEST)
    b = jnp.where(col_valid[None, None, :, None, :], b, NEG)
    return b.reshape(rpb.shape[0], len(pat_rows), GRID_W, kh * GRID_W)


def _neighbourhood_attention(slabs, gq, gk, rpb, B, S):
    HP = HEAD_PAIRS
    rows = S // GRID_W
    assert rows >= NA_ROWS
    n_pat = NA_ROWS
    bias = _na_bias_table(rpb, rows).reshape(HP, 2, n_pat, GRID_W, NA_ROWS * GRID_W)
    g2q = jnp.tile(gq.astype(F32), 2).reshape(1, LANES)
    g2k = jnp.tile(gk.astype(F32), 2).reshape(1, LANES)
    x = slabs.reshape(slabs.shape[0], B, S, LANES)
    spec = lambda g: pl.BlockSpec((None, 1, S, LANES), lambda h, b, g=g: (g * HP + h, b, 0, 0))
    vec = pl.BlockSpec((1, LANES), lambda h, b: (0, 0))
    return pl.pallas_call(
        functools.partial(_na_kernel, rows=rows), grid=(HP, B),
        in_specs=[spec(0), spec(1), spec(2), vec, vec,
                  pl.BlockSpec((None, 2, n_pat, GRID_W, NA_ROWS * GRID_W), lambda h, b: (h, 0, 0, 0, 0))],
        out_specs=pl.BlockSpec((None, 1, S, LANES), lambda h, b: (h, b, 0, 0)),
        out_shape=jax.ShapeDtypeStruct((HP, B, S, LANES), F32),
        scratch_shapes=[pltpu.VMEM((S, LANES), BF16), pltpu.VMEM((S, LANES), BF16)],
        compiler_params=pltpu.CompilerParams(dimension_semantics=("arbitrary", "arbitrary"),
                                             vmem_limit_bytes=VMEM_LIMIT),
        name="nbr_attention")(x, x, x, g2q, g2k, bias)


def _dil_kernel(q_ref, k_ref, v_ref, gq_ref, gk_ref, cos_ref, sin_ref, o_ref,
                qn_ref, kn_ref, qd_ref, kd_ref, vd_ref, od_ref, ld_ref, os_ref, ls_ref):
    S = q_ref.shape[1]
    BLK = DIL_BLK
    BQ = DIL_BQ
    NK = BQ + 2 * BLK
    cos = cos_ref[...]
    sin = sin_ref[...]
    qn_ref[...] = _rope2(_head_rms2(q_ref[0], gq_ref[...]), cos, sin) * HEAD_DIM ** -0.5
    kn_ref[...] = _rope2(_head_rms2(k_ref[0], gk_ref[...]), cos, sin)
    first = lax.broadcasted_iota(jnp.int32, (BQ, LANES), 1) < HEAD_DIM
    qi = lax.broadcasted_iota(jnp.int32, (BQ, NK), 0)
    kc = lax.broadcasted_iota(jnp.int32, (BQ, NK), 1) - BLK
    band = jnp.abs(kc - qi) <= BLK
    for bi, (window, dil) in enumerate(DIL_BRANCHES):
        assert window // (2 * dil) == BLK
        Ls = S // dil
        shift = Ls.bit_length() - 1
        assert 1 << shift == Ls and Ls >= 2 * BLK
        for r in range(dil):
            sl = pl.ds(r, Ls, stride=dil) if dil > 1 else pl.ds(0, Ls)
            qd_ref[pl.ds(r * Ls, Ls), :] = qn_ref[sl, :].astype(BF16)
            kd_ref[pl.ds(r * Ls, Ls), :] = kn_ref[sl, :].astype(BF16)
            vd_ref[pl.ds(r * Ls, Ls), :] = v_ref.at[0][sl, :].astype(BF16)

        def body(fb, c):
            row0 = pl.multiple_of(fb * BQ, BQ)
            prev0 = pl.multiple_of(jnp.maximum(row0 - BLK, 0), BLK)
            next0 = pl.multiple_of(jnp.minimum(row0 + BQ, S - BLK), BLK)
            kk = jnp.concatenate([kd_ref[pl.ds(prev0, BLK), :], kd_ref[pl.ds(row0, BQ), :],
                                  kd_ref[pl.ds(next0, BLK), :]], axis=0)
            vv = jnp.concatenate([vd_ref[pl.ds(prev0, BLK), :], vd_ref[pl.ds(row0, BQ), :],
                                  vd_ref[pl.ds(next0, BLK), :]], axis=0)
            valid = band & (((row0 + kc) >> shift) == ((row0 + qi) >> shift))
            q2 = qd_ref[pl.ds(row0, BQ), :]
            outs, lses = [], []
            for hh in range(2):
                qm = jnp.where(first if hh == 0 else ~first, q2, jnp.zeros_like(q2))
                s = lax.dot_general(qm, kk, (((1,), (1,)), ((), ())), preferred_element_type=F32)
                s = jnp.where(valid, s, NEG)
                m = jnp.max(s, axis=-1, keepdims=True)
                p = jnp.exp(s - m)
                den = jnp.sum(p, axis=-1, keepdims=True)
                outs.append(jnp.dot(p.astype(BF16), vv, preferred_element_type=F32) / den)
                lses.append(m + jnp.log(den))
            od_ref[pl.ds(row0, BQ), :] = jnp.where(first, outs[0], outs[1])
            ld_ref[pl.ds(row0, BQ), :] = jnp.where(first, lses[0], lses[1])
            return c

        lax.fori_loop(0, S // BQ, body, 0)
        for r in range(dil):
            sl = pl.ds(r, Ls, stride=dil) if dil > 1 else pl.ds(0, Ls)
            os_ref.at[bi][sl, :] = od_ref[pl.ds(r * Ls, Ls), :]
            ls_ref.at[bi][sl, :] = ld_ref[pl.ds(r * Ls, Ls), :]
    l0, l1, l2 = ls_ref[0], ls_ref[1], ls_ref[2]
    mx = jnp.maximum(jnp.maximum(l0, l1), l2)
    w0, w1, w2 = jnp.exp(l0 - mx), jnp.exp(l1 - mx), jnp.exp(l2 - mx)
    o_ref[0] = (w0 * os_ref[0] + w1 * os_ref[1] + w2 * os_ref[2]) / (w0 + w1 + w2)


def _dilated_attention(slabs, first_slab, gq, gk, cos, sin, B, S):
    HP = HEAD_PAIRS
    x = slabs.reshape(slabs.shape[0], B, S, LANES)
    g2q = jnp.tile(gq.astype(F32), 2).reshape(1, LANES)
    g2k = jnp.tile(gk.astype(F32), 2).reshape(1, LANES)
    spec = lambda g: pl.BlockSpec((None, 1, S, LANES), lambda h, b, g=g: (first_slab + g * HP + h, b, 0, 0))
    tab = pl.BlockSpec((S, LANES), lambda h, b: (0, 0))
    vec = pl.BlockSpec((1, LANES), lambda h, b: (0, 0))
    return pl.pallas_call(
        _dil_kernel, grid=(HP, B),
        in_specs=[spec(0), spec(1), spec(2), vec, vec, tab, tab],
        out_specs=pl.BlockSpec((None, 1, S, LANES), lambda h, b: (h, b, 0, 0)),
        out_shape=jax.ShapeDtypeStruct((HP, B, S, LANES), F32),
        scratch_shapes=[pltpu.VMEM((S, LANES), F32), pltpu.VMEM((S, LANES), F32),
                        pltpu.VMEM((S, LANES), BF16), pltpu.VMEM((S, LANES), BF16),
                        pltpu.VMEM((S, LANES), BF16),
                        pltpu.VMEM((S, LANES), F32), pltpu.VMEM((S, LANES), F32),
                        pltpu.VMEM((len(DIL_BRANCHES), S, LANES), F32),
                        pltpu.VMEM((len(DIL_BRANCHES), S, LANES), F32)],
        compiler_params=pltpu.CompilerParams(dimension_semantics=("arbitrary", "arbitrary"),
                                             vmem_limit_bytes=VMEM_LIMIT),
        name="dilated_attention")(x, x, x, g2q, g2k, cos, sin)


def _stair_tables():
    a_idx = [0] * 16 + [a for a in range(1, 8) for _ in range(8)] + list(range(8, 16))
    b_idx = list(range(16)) + [b for _ in range(1, 8) for b in range(8)] + [0] * 8
    a_idx = np.array(a_idx)
    b_idx = np.array(b_idx)
    valid = (a_idx + 1) * (b_idx + 1) <= PEER_TOPK
    return a_idx, b_idx, valid


_ST_A, _ST_B, _ST_VALID = _stair_tables()
N_CAND = len(_ST_A)


def _top16_rows(sc, iota):
    vals, idxs = [], []
    big = float(sc.shape[0])
    for r in range(PEER_TOPK):
        m = jnp.max(sc, axis=0, keepdims=True)
        idx = jnp.min(jnp.where(sc == m, iota, big), axis=0, keepdims=True)
        vals.append(m)
        idxs.append(idx)
        if r + 1 < PEER_TOPK:
            sc = jnp.where(iota == idx, NEG_INF, sc)
    return jnp.concatenate(vals, axis=0), jnp.concatenate(idxs, axis=0)


def _route_kernel(xn_ref, wqT_ref, keys_ref, flat_ref, e_ref, g_ref, qT_ref, eT_ref, gT_ref):
    Tb = xn_ref.shape[0]
    xb = xn_ref[...].astype(BF16)
    qT_ref[...] = lax.dot_general(wqT_ref[...], xb, (((1,), (1,)), ((), ())), preferred_element_type=F32)
    iota = lax.broadcasted_iota(jnp.int32, (PEER_KEYS, Tb), 0).astype(F32)
    flat = jnp.broadcast_to(flat_ref[...], (N_CAND, Tb))
    for h in range(PEER_HEADS):
        s, ix = [], []
        for p in range(2):
            r0 = (h * 2 + p) * (PEER_DK // 2)
            sc = jnp.dot(keys_ref[h * 2 + p], qT_ref[pl.ds(r0, PEER_DK // 2), :].astype(BF16),
                         preferred_element_type=F32)
            sp, ip = _top16_rows(sc, iota)
            s.append(sp)
            ix.append(ip)
        s0, s1 = s
        i0, i1 = ix[0] * float(PEER_KEYS), ix[1]
        parts_s = [s0[0:1] + s1]
        parts_e = [i0[0:1] + i1]
        for a in range(1, 8):
            parts_s.append(s0[a:a + 1] + s1[0:8])
            parts_e.append(i0[a:a + 1] + i1[0:8])
        parts_s.append(s0[8:16] + s1[0:1])
        parts_e.append(i0[8:16] + i1[0:1])
        cand = jnp.where(flat < 256.0, jnp.concatenate(parts_s, axis=0), NEG_INF)
        eid = jnp.concatenate(parts_e, axis=0)
        gs, es = [], []
        for r in range(PEER_TOPK):
            m = jnp.max(cand, axis=0, keepdims=True)
            fsel = jnp.min(jnp.where(cand == m, flat, float(1 << 20)), axis=0, keepdims=True)
            hit = flat == fsel
            es.append(jnp.max(jnp.where(hit, eid, -1.0), axis=0, keepdims=True))
            gs.append(m)
            if r + 1 < PEER_TOPK:
                cand = jnp.where(hit, NEG_INF, cand)
        g = jnp.concatenate(gs, axis=0)
        ex = jnp.exp(g - g[0:1])
        eT_ref[pl.ds(h * PEER_TOPK, PEER_TOPK), :] = jnp.concatenate(es, axis=0).astype(jnp.int32)
        gT_ref[pl.ds(h * PEER_TOPK, PEER_TOPK), :] = ex / jnp.sum(ex, axis=0, keepdims=True)
    e_ref[...] = eT_ref[...].T
    g_ref[...] = gT_ref[...].T


def _peer_route(xn, w_q, sub_keys):
    T = xn.shape[0]
    Tb = ROUTE_TOK
    wqT = w_q.T.astype(BF16)
    keys = sub_keys.reshape(PEER_HEADS * 2, PEER_KEYS, PEER_DK // 2).astype(BF16)
    flat = np.where(_ST_VALID, _ST_A * 16 + _ST_B, 1 << 16).astype(np.float32).reshape(N_CAND, 1)
    return pl.pallas_call(
        _route_kernel, grid=(T // Tb,),
        in_specs=[pl.BlockSpec((Tb, D_MODEL), lambda i: (i, 0)),
                  pl.BlockSpec((D_MODEL, D_MODEL), lambda i: (0, 0)),
                  pl.BlockSpec((PEER_HEADS * 2, PEER_KEYS, PEER_DK // 2), lambda i: (0, 0, 0)),
                  pl.BlockSpec((N_CAND, 1), lambda i: (0, 0))],
        out_specs=[pl.BlockSpec((Tb, PEER_SEL), lambda i: (i, 0)),
                   pl.BlockSpec((Tb, PEER_SEL), lambda i: (i, 0))],
        out_shape=[jax.ShapeDtypeStruct((T, PEER_SEL), jnp.int32),
                   jax.ShapeDtypeStruct((T, PEER_SEL), F32)],
        scratch_shapes=[pltpu.VMEM((D_MODEL, Tb), F32), pltpu.VMEM((PEER_SEL, Tb), jnp.int32),
                        pltpu.VMEM((PEER_SEL, Tb), F32)],
        compiler_params=pltpu.CompilerParams(dimension_semantics=("arbitrary",),
                                             vmem_limit_bytes=VMEM_LIMIT),
        name="peer_route")(xn, wqT, keys, jnp.asarray(flat))


def _peer_eval_kernel(idx_cur, idx_nxt, x_ref, r_ref, g_ref, uv_hbm, o_ref, buf, sem):
    i = pl.program_id(0)
    n = pl.num_programs(0)
    G = PEER_TOK
    NS = PEER_SLOTS
    assert PEER_AHEAD == NS - 1

    def issue_tok(idx_ref, row, slot, g):
        for k in range(PEER_SEL):
            e = idx_ref[row, k]
            pltpu.make_async_copy(uv_hbm.at[e], buf.at[slot, g, k // SUBLANES, :, k % SUBLANES, :],
                                  sem.at[slot]).start(priority=k % DMA_THREADS)

    def wait_slot(slot):
        pltpu.make_async_copy(buf.at[slot], buf.at[slot], sem.at[slot]).wait()

    def compute_tok(slot, g, row, gates_t):
        xg = x_ref[row]
        acc = buf[slot, g, :, 0].reshape(PEER_SEL, LANES) * xg[0:1, :]
        for c in range(1, SUBLANES):
            acc = acc + buf[slot, g, :, c].reshape(PEER_SEL, LANES) * xg[c:c + 1, :]
        h = jnp.sum(acc, axis=-1, keepdims=True)
        a = jax.nn.gelu(h) * gates_t[:, row:row + 1]
        ab = jnp.broadcast_to(a, (PEER_SEL, LANES))
        rows = [jnp.sum(ab * buf[slot, g, :, SUBLANES + c].reshape(PEER_SEL, LANES), axis=0, keepdims=True)
                for c in range(SUBLANES)]
        o_ref[row] = r_ref[row] + jnp.concatenate(rows, axis=0)

    @pl.when(i == 0)
    def _():
        for j in range(PEER_AHEAD):
            for g in range(G):
                issue_tok(idx_cur, j * G + g, j, g)

    gates_t = g_ref[...].T
    for j in range(NS):
        wait_slot(j)
        ahead = j + PEER_AHEAD
        for g in range(G):
            if ahead < NS:
                issue_tok(idx_cur, ahead * G + g, ahead, g)
            else:
                issue_tok(idx_nxt, (ahead - NS) * G + g, ahead - NS, g)
            compute_tok(j, g, j * G + g, gates_t)

    @pl.when(i == n - 1)
    def _():
        for j in range(PEER_AHEAD):
            wait_slot(j)


def _peer_eval(xt, res, experts, gates, uv):
    T = xt.shape[0]
    G = PEER_TOK
    NT = PEER_SLOTS * G
    n = T // NT
    x3 = xt.reshape(T, SUBLANES, LANES)
    r3 = res.reshape(T, SUBLANES, LANES)
    tok = pl.BlockSpec((NT, SUBLANES, LANES), lambda i: (i, 0, 0))
    out = pl.pallas_call(
        _peer_eval_kernel,
        grid=(n,),
        in_specs=[
            pl.BlockSpec((NT, PEER_SEL), lambda i: (i, 0), memory_space=pltpu.SMEM),
            pl.BlockSpec((NT, PEER_SEL), lambda i: (jnp.minimum(i + 1, n - 1), 0),
                         memory_space=pltpu.SMEM),
            tok,
            tok,
            pl.BlockSpec((NT, PEER_SEL), lambda i: (i, 0)),
            pl.BlockSpec(memory_space=pl.ANY),
        ],
        out_specs=tok,
        out_shape=jax.ShapeDtypeStruct((T, SUBLANES, LANES), F32),
        scratch_shapes=[
            pltpu.VMEM((PEER_SLOTS, G, 2 * SUBLANES, 2 * SUBLANES, SUBLANES, LANES), F32),
            pltpu.SemaphoreType.DMA((PEER_SLOTS,)),
        ],
        compiler_params=pltpu.CompilerParams(
            dimension_semantics=("arbitrary",),
            vmem_limit_bytes=VMEM_LIMIT),
        name="peer_eval",
    )(experts, experts, x3, r3, gates, uv)
    return out.reshape(T, D_MODEL)


def _peer(xn, res, w_q, sub_keys, u, v):
    E = u.shape[0]
    experts, gates = _peer_route(xn, w_q, sub_keys)
    uv = jnp.concatenate([u.astype(F32).reshape(E, SUBLANES, LANES),
                          v.astype(F32).reshape(E, SUBLANES, LANES)], axis=1)
    return _peer_eval(xn, res, experts, gates, uv)


def kernel(x, mix_norm_gain, ffn_norm_gain, ev_w_in, ev_w_out, hy_conv_w, hy_conv_b, hy_fw1, hy_fb1, hy_fw2, hy_fb2, hy_fw3, hy_fb3, hy_fw_out, hy_freq, hy_skip, ret_log_decay, ret_gn_gain, od_w_in, od_w_out, na_qk_gain, na_rpb, dil_qk_gain, peer_w_q, peer_sub_keys, peer_u, peer_v):
    B, S, D = x.shape
    T = B * S
    depth = mix_norm_gain.shape[0]
    cos, sin = _rope_tables(S)
    emb, window = _filter_consts(S)
    dr, di, wk = _dft_tables(S)
    xt = x.reshape(T, D).astype(F32)
    for layer in range(depth):
        if layer % 2 == 0:
            e = layer // 2
            slabs = _norm_in_proj(xt, mix_norm_gain[layer], ev_w_in[e])
            hr, hi = _hyena_filter_spectrum(hy_fw1[e], hy_fb1[e], hy_fw2[e], hy_fb2[e], hy_fw3[e], hy_fb3[e],
                                            hy_fw_out[e], hy_freq[e], emb, window, dr, di)
            a_out = _hyena_conv(slabs, hy_conv_w[e], hy_conv_b[e], hy_skip[e], hr, hi, dr, di, wk, B, S)
            b_out = _retention(slabs, 3 * HEAD_PAIRS, ret_log_decay[e], ret_gn_gain[e], cos, sin, B, S)
            w_out = ev_w_out[e]
        else:
            o = layer // 2
            slabs = _norm_in_proj(xt, mix_norm_gain[layer], od_w_in[o])
            a_out = _neighbourhood_attention(slabs, na_qk_gain[o, 0], na_qk_gain[o, 1], na_rpb[o], B, S)
            b_out = _dilated_attention(slabs, 3 * HEAD_PAIRS, dil_qk_gain[o, 0], dil_qk_gain[o, 1],
                                       cos, sin, B, S)
            w_out = od_w_out[o]
        xt, xn = _out_proj_residual(a_out.reshape(HEAD_PAIRS, T, LANES), b_out.reshape(HEAD_PAIRS, T, LANES),
                                    w_out, xt, ffn_norm_gain[layer])
        xt = _peer(xn, xt, peer_w_q[layer], peer_sub_keys[layer], peer_u[layer], peer_v[layer])
    return xt.reshape(B, S, D).astype(x.dtype)
```

```python
import functools
import math

import numpy as np
import jax
import jax.numpy as jnp
from jax import lax
from jax.experimental import pallas as pl
from jax.experimental.pallas import tpu as pltpu

F32 = jnp.float32
BF16 = jnp.bfloat16

D_MODEL = 1024
HEAD_DIM = 64
GROUP_WIDTH = D_MODEL // 2
EPS = 1e-6
NEG = -1e30
NEG_INF = float("-inf")
ROPE_THETA = 10000.0

HY_CHANNELS = GROUP_WIDTH
HY_ORDER = 2
HY_EMB = 33
HY_EMB_PAD = 40
HY_BANDS = (HY_EMB - 1) // 2
HY_HID = 64
HY_FAST_DECAY = 0.3
HY_SLOW_DECAY = 1.5
HY_DECAY_TARGET = 1e-2
HY_MAX_DECAY = math.log(HY_DECAY_TARGET) / HY_FAST_DECAY
HY_MIN_DECAY = math.log(HY_DECAY_TARGET) / HY_SLOW_DECAY
HY_MOD_SHIFT = 0.05

GRID_W = 64
NA_HEADS = GROUP_WIDTH // HEAD_DIM
NA_ROWS = 8
NA_COLS = 16

DIL_BRANCHES = ((128, 1), (512, 4), (2048, 16))
DIL_BLK = 64
DIL_BQ = 256

PEER_HEADS = 8
PEER_KEYS = 128
PEER_DK = 128
PEER_TOPK = 16
PEER_SEL = PEER_HEADS * PEER_TOPK

SUBLANES = 8
LANES = 128
assert D_MODEL == SUBLANES * LANES
HEAD_PAIRS = GROUP_WIDTH // LANES
PEER_TOK = 4
PEER_SLOTS = 4
PEER_AHEAD = PEER_SLOTS - 1
ROUTE_TOK = 256
DMA_THREADS = 2
MM_TOK = 256
MM_NCHUNK = 512
RET_BQ = 256
HY_CB = 128
HY_FILT_CB = 256
VMEM_LIMIT = 48 * 1024 * 1024
VMEM_LIMIT_HYENA = 56 * 1024 * 1024


def _split(x):
    hi = x.astype(BF16)
    return hi, (x - hi.astype(F32)).astype(BF16)


def _dot3(a, b):
    ah, al = _split(a)
    bh, bl = _split(b)
    d = lambda x, y: jnp.dot(x, y, preferred_element_type=F32)
    return d(ah, bh) + (d(ah, bl) + d(al, bh))


def _rms_rows(x, gain):
    return x * lax.rsqrt(jnp.mean(x * x, axis=-1, keepdims=True) + EPS) * gain


def _pair_sums(x):
    half = (lax.broadcasted_iota(jnp.int32, (LANES, LANES), 0) // HEAD_DIM ==
            lax.broadcasted_iota(jnp.int32, (LANES, LANES), 1) // HEAD_DIM).astype(BF16)
    hi, lo = _split(x)
    return (jnp.dot(hi, half, preferred_element_type=F32) +
            jnp.dot(lo, half, preferred_element_type=F32))


def _head_rms2(t, gain2):
    ms = _pair_sums(t * t) * (1.0 / HEAD_DIM)
    return t * lax.rsqrt(ms + EPS) * gain2


def _rope_tables(S):
    lane = np.arange(LANES)
    inv = ROPE_THETA ** (-jnp.arange(0, HEAD_DIM, 2, dtype=F32) / HEAD_DIM)
    ang = jnp.arange(S, dtype=F32)[:, None] * inv[None, :]
    cos = jnp.tile(jnp.cos(ang), (1, LANES // (HEAD_DIM // 2)))
    sin = jnp.tile(jnp.sin(ang), (1, LANES // (HEAD_DIM // 2)))
    sign = np.where((lane % HEAD_DIM) < HEAD_DIM // 2, -1.0, 1.0).astype(np.float32)
    return cos, sin * sign[None, :]


def _rope2(t, cos, sin_signed):
    lane = lax.broadcasted_iota(jnp.int32, t.shape, 1)
    lower = (lane % HEAD_DIM) < HEAD_DIM // 2
    partner = jnp.where(lower, pltpu.roll(t, LANES - HEAD_DIM // 2, 1), pltpu.roll(t, HEAD_DIM // 2, 1))
    return t * cos + partner * sin_signed


def _mm_in_kernel(x_ref, g_ref, w_ref, o_ref):
    xn = _rms_rows(x_ref[...], g_ref[...]).astype(BF16)
    per = MM_NCHUNK // LANES
    for c in range(o_ref.shape[0] // per):
        y = jnp.dot(xn, w_ref[:, c * MM_NCHUNK:(c + 1) * MM_NCHUNK], preferred_element_type=F32)
        for s in range(per):
            o_ref[c * per + s] = y[:, s * LANES:(s + 1) * LANES]


def _norm_in_proj(x, gain, w):
    T, K = x.shape
    N = w.shape[1]
    assert N % MM_NCHUNK == 0
    return pl.pallas_call(
        _mm_in_kernel, grid=(T // MM_TOK,),
        in_specs=[pl.BlockSpec((MM_TOK, K), lambda i: (i, 0)),
                  pl.BlockSpec((1, K), lambda i: (0, 0)),
                  pl.BlockSpec((K, N), lambda i: (0, 0))],
        out_specs=pl.BlockSpec((N // LANES, MM_TOK, LANES), lambda i: (0, i, 0)),
        out_shape=jax.ShapeDtypeStruct((N // LANES, T, LANES), F32),
        compiler_params=pltpu.CompilerParams(dimension_semantics=("arbitrary",),
                                             vmem_limit_bytes=VMEM_LIMIT),
        name="norm_in_proj")(x, gain.astype(F32).reshape(1, K), w.astype(BF16))


def _mm_out_kernel(a_ref, b_ref, w_ref, x_ref, g_ref, xo_ref, xn_ref):
    lhs = jnp.concatenate([a_ref[s].astype(BF16) for s in range(a_ref.shape[0])] +
                          [b_ref[s].astype(BF16) for s in range(b_ref.shape[0])], axis=-1)
    xo = x_ref[...] + jnp.dot(lhs, w_ref[...], preferred_element_type=F32)
    xo_ref[...] = xo
    xn_ref[...] = _rms_rows(xo, g_ref[...])


def _out_proj_residual(a_slabs, b_slabs, w, x, gain):
    NA_, T, _ = a_slabs.shape
    NB_ = b_slabs.shape[0]
    N = w.shape[1]
    row = pl.BlockSpec((MM_TOK, N), lambda i: (i, 0))
    return pl.pallas_call(
        _mm_out_kernel, grid=(T // MM_TOK,),
        in_specs=[pl.BlockSpec((NA_, MM_TOK, LANES), lambda i: (0, i, 0)),
                  pl.BlockSpec((NB_, MM_TOK, LANES), lambda i: (0, i, 0)),
                  pl.BlockSpec(((NA_ + NB_) * LANES, N), lambda i: (0, 0)),
                  row,
                  pl.BlockSpec((1, N), lambda i: (0, 0))],
        out_specs=[row, row],
        out_shape=[jax.ShapeDtypeStruct((T, N), F32), jax.ShapeDtypeStruct((T, N), F32)],
        compiler_params=pltpu.CompilerParams(dimension_semantics=("arbitrary",),
                                             vmem_limit_bytes=VMEM_LIMIT),
        name="out_proj_residual")(a_slabs, b_slabs, w.astype(BF16), x, gain.astype(F32).reshape(1, N))


def _dft_tables(L):
    N = 2 * L
    NF = -(-(L + 1) // LANES) * LANES
    k = jnp.arange(NF, dtype=jnp.int32)[:, None]
    n = jnp.arange(L, dtype=jnp.int32)[None, :]
    ang = ((k * n) % N).astype(F32) * (2.0 * math.pi / N)
    live = k <= L
    fr = jnp.where(live, jnp.cos(ang), 0.0).astype(BF16)
    fi = jnp.where(live, -jnp.sin(ang), 0.0).astype(BF16)
    kk = np.arange(NF)
    wk = np.where((kk == 0) | (kk == L), 1.0, np.where(kk < L, 2.0, 0.0)) / N
    return fr, fi, jnp.asarray(wk.astype(np.float32).reshape(NF, 1))


def _filter_consts(L):
    t_norm = jnp.linspace(0.0, 1.0, L, dtype=F32)[:, None]
    w = 2.0 * math.pi * jnp.arange(L, dtype=F32)[:, None] / L
    f = jnp.linspace(1e-4, HY_BANDS - 1, HY_BANDS, dtype=F32)[None, :]
    emb = jnp.concatenate([t_norm, jnp.cos(f * w), -jnp.sin(f * w)], axis=-1)
    emb = jnp.pad(emb, ((0, 0), (0, HY_EMB_PAD - HY_EMB)))
    deltas = jnp.abs(jnp.linspace(HY_MIN_DECAY, HY_MAX_DECAY, HY_CHANNELS, dtype=F32))
    window = jnp.exp(-t_norm * deltas[None, :]) + HY_MOD_SHIFT
    return emb, window


def _filt_kernel(emb_ref, w1_ref, b1_ref, w2_ref, b2_ref, w3_ref, b3_ref, wo_ref, freq_ref, win_ref,
                 dr_ref, di_ref, hr_ref, hi_ref):
    fr = freq_ref[...]
    h = jnp.sin(fr * (_dot3(emb_ref[...], w1_ref[...]) + b1_ref[...]))
    h = jnp.sin(fr * (_dot3(h, w2_ref[...]) + b2_ref[...]))
    h = jnp.sin(fr * (_dot3(h, w3_ref[...]) + b3_ref[...]))
    win = win_ref[...]
    fwd = _dot3(h, wo_ref[0, 0]) * win
    bwd = _dot3(h, wo_ref[0, 1]) * win
    row = lax.broadcasted_iota(jnp.int32, bwd.shape, 0)
    bwd = jnp.where(row == 0, 0.0, bwd)
    sh, sl = _split(fwd + bwd)
    dh, dl = _split(fwd - bwd)
    d = lambda x, y: jnp.dot(x, y, preferred_element_type=F32)
    hr_ref[0] = d(dr_ref[...], sh) + d(dr_ref[...], sl)
    hi_ref[0] = d(di_ref[...], dh) + d(di_ref[...], dl)


def _hyena_filter_spectrum(fw1, fb1, fw2, fb2, fw3, fb3, fw_out, freq, emb, window, dr, di):
    L = emb.shape[0]
    NF = dr.shape[0]
    C = HY_CHANNELS
    w1 = jnp.pad(fw1.astype(F32), ((0, HY_EMB_PAD - HY_EMB), (0, 0)))
    wo = fw_out.astype(F32).reshape(HY_HID, HY_ORDER, 2, C).transpose(1, 2, 0, 3)
    vec = lambda v: v.astype(F32).reshape(1, HY_HID)
    full = lambda a: pl.BlockSpec(a.shape, lambda o, c: (0,) * a.ndim)
    args = (emb, w1, vec(fb1), fw2.astype(F32), vec(fb2), fw3.astype(F32), vec(fb3))
    once = pl.Buffered(1)
    out = pl.BlockSpec((1, NF, HY_FILT_CB), lambda o, c: (o, 0, c))
    return pl.pallas_call(
        _filt_kernel, grid=(HY_ORDER, C // HY_FILT_CB),
        in_specs=[full(a) for a in args] + [
            pl.BlockSpec((1, 2, HY_HID, HY_FILT_CB), lambda o, c: (o, 0, 0, c)),
            pl.BlockSpec((1, HY_HID), lambda o, c: (0, 0)),
            pl.BlockSpec((L, HY_FILT_CB), lambda o, c: (0, c)),
            pl.BlockSpec((NF, L), lambda o, c: (0, 0), pipeline_mode=once),
            pl.BlockSpec((NF, L), lambda o, c: (0, 0), pipeline_mode=once)],
        out_specs=[out, out],
        out_shape=[jax.ShapeDtypeStruct((HY_ORDER, NF, C), F32)] * 2,
        compiler_params=pltpu.CompilerParams(dimension_semantics=("arbitrary", "arbitrary"),
                                             vmem_limit_bytes=VMEM_LIMIT_HYENA),
        name="hyena_filter")(*args, wo, vec(freq), window, dr, di)


def _conv_kernel(v_ref, x1_ref, x2_ref, cw_ref, cb_ref, skip_ref, hr_ref, hi_ref, dr_ref, di_ref, wk_ref, o_ref):
    S = v_ref.shape[2]
    nsl = v_ref.shape[0]
    row = lax.broadcasted_iota(jnp.int32, (S, LANES), 0)

    def short(u_ref, g):
        cols = []
        for s in range(nsl):
            u = u_ref[s, 0]
            w = cw_ref[g, s]
            up = jnp.where(row == 0, 0.0, pltpu.roll(u, 1, 0))
            un = jnp.where(row == S - 1, 0.0, pltpu.roll(u, S - 1, 0))
            cols.append(up * w[0:1] + u * w[1:2] + un * w[2:3] + cb_ref[g, s])
        return jnp.concatenate(cols, axis=-1)

    d = lambda x, y: jnp.dot(x, y, preferred_element_type=F32)
    wk = wk_ref[...]

    def long_conv(z, o):
        zb = z.astype(BF16)
        zr = d(dr_ref[...], zb)
        zi = d(di_ref[...], zb)
        hr = hr_ref[o]
        hi = hi_ref[o]
        yr = ((zr * hr - zi * hi) * wk).T.astype(BF16)
        yi = ((zr * hi + zi * hr) * wk).T.astype(BF16)
        y = d(yr, dr_ref[...]) + d(yi, di_ref[...])
        return y.T + z * skip_ref[o]

    v = short(v_ref, 0)
    z = short(x1_ref, 1) * long_conv(v, 0)
    out = short(x2_ref, 2) * long_conv(z, 1)
    for s in range(nsl):
        o_ref[s, 0] = out[:, s * LANES:(s + 1) * LANES]


def _hyena_conv(slabs, conv_w, conv_b, skip, hr, hi, dr, di, wk, B, S):
    C = HY_CHANNELS
    NF = dr.shape[0]
    nsl = HY_CB // LANES
    nc = C // HY_CB
    x = slabs.reshape(slabs.shape[0], B, S, LANES)
    cw = conv_w.astype(F32).reshape(3, 3, C // LANES, LANES).transpose(1, 2, 0, 3)
    cb = conv_b.astype(F32).reshape(3, C // LANES, 1, LANES)
    sk = skip.astype(F32).reshape(HY_ORDER, 1, C)
    spec = lambda g: pl.BlockSpec((nsl, 1, S, LANES), lambda c, b, g=g: (g * nc + c, b, 0, 0))
    once = pl.Buffered(1)
    return pl.pallas_call(
        _conv_kernel, grid=(nc, B),
        in_specs=[spec(0), spec(1), spec(2),
                  pl.BlockSpec((3, nsl, 3, LANES), lambda c, b: (0, c, 0, 0)),
                  pl.BlockSpec((3, nsl, 1, LANES), lambda c, b: (0, c, 0, 0)),
                  pl.BlockSpec((HY_ORDER, 1, HY_CB), lambda c, b: (0, 0, c)),
                  pl.BlockSpec((HY_ORDER, NF, HY_CB), lambda c, b: (0, 0, c), pipeline_mode=once),
                  pl.BlockSpec((HY_ORDER, NF, HY_CB), lambda c, b: (0, 0, c), pipeline_mode=once),
                  pl.BlockSpec((NF, S), lambda c, b: (0, 0), pipeline_mode=once),
                  pl.BlockSpec((NF, S), lambda c, b: (0, 0), pipeline_mode=once),
                  pl.BlockSpec((NF, 1), lambda c, b: (0, 0))],
        out_specs=pl.BlockSpec((nsl, 1, S, LANES), lambda c, b: (c, b, 0, 0)),
        out_shape=jax.ShapeDtypeStruct((C // LANES, B, S, LANES), F32),
        compiler_params=pltpu.CompilerParams(dimension_semantics=("arbitrary", "arbitrary"),
                                             vmem_limit_bytes=VMEM_LIMIT_HYENA),
        name="hyena_conv")(x, x, x, cw, cb, sk, hr, hi, dr, di, wk)


def _ret_kernel(lg_ref, q_ref, k_ref, v_ref, g_ref, cos_ref, sin_ref, gn_ref, o_ref, qr_ref, kr_ref, vb_ref):
    hp = pl.program_id(0)
    S = q_ref.shape[1]
    cos = cos_ref[...]
    sin = sin_ref[...]
    qr_ref[...] = _rope2(q_ref[0], cos, sin).astype(BF16)
    kr_ref[...] = (_rope2(k_ref[0], cos, sin) * HEAD_DIM ** -0.5).astype(BF16)
    vb_ref[...] = v_ref[0].astype(BF16)
    first = lax.broadcasted_iota(jnp.int32, (RET_BQ, LANES), 1) < HEAD_DIM
    m_idx = lax.broadcasted_iota(jnp.int32, (RET_BQ, S), 1)
    n_loc = lax.broadcasted_iota(jnp.int32, (RET_BQ, S), 0)

    def body(r, c):
        row0 = pl.multiple_of(r * RET_BQ, RET_BQ)
        diff = (n_loc + row0 - m_idx).astype(F32)
        q2 = qr_ref[pl.ds(row0, RET_BQ), :]
        ys = []
        for hh in range(2):
            lgf = lg_ref[0, 2 * hp + hh]
            lgb = lg_ref[1, 2 * hp + hh]
            dec = jnp.exp(diff * jnp.where(diff >= 0.0, lgf, -lgb))
            qm = jnp.where(first if hh == 0 else ~first, q2, jnp.zeros_like(q2))
            s = lax.dot_general(qm, kr_ref[...], (((1,), (1,)), ((), ())), preferred_element_type=F32)
            ys.append(jnp.dot((s * dec).astype(BF16), vb_ref[...], preferred_element_type=F32))
        y = jnp.where(first, ys[0], ys[1])
        y = y * lax.rsqrt(_pair_sums(y * y) * (1.0 / HEAD_DIM) + EPS) * gn_ref[0]
        g = g_ref[0, pl.ds(row0, RET_BQ), :]
        o_ref[0, pl.ds(row0, RET_BQ), :] = g * jax.nn.sigmoid(g) * y
        return c

    lax.fori_loop(0, S // RET_BQ, body, 0)


def _retention(slabs, first_slab, log_decay_param, gn_gain, cos, sin, B, S):
    HP = HEAD_PAIRS
    x = slabs.reshape(slabs.shape[0], B, S, LANES)
    lg = -jnp.exp(log_decay_param.astype(F32))
    gn = gn_gain.astype(F32).reshape(HP, 1, LANES)
    spec = lambda g: pl.BlockSpec((None, 1, S, LANES), lambda h, b, g=g: (first_slab + g * HP + h, b, 0, 0))
    tab = pl.BlockSpec((S, LANES), lambda h, b: (0, 0))
    return pl.pallas_call(
        _ret_kernel, grid=(HP, B),
        in_specs=[pl.BlockSpec(memory_space=pltpu.SMEM), spec(0), spec(1), spec(2), spec(3), tab, tab,
                  pl.BlockSpec((None, 1, LANES), lambda h, b: (h, 0, 0))],
        out_specs=pl.BlockSpec((None, 1, S, LANES), lambda h, b: (h, b, 0, 0)),
        out_shape=jax.ShapeDtypeStruct((HP, B, S, LANES), F32),
        scratch_shapes=[pltpu.VMEM((S, LANES), BF16)] * 3,
        compiler_params=pltpu.CompilerParams(dimension_semantics=("arbitrary", "arbitrary"),
                                             vmem_limit_bytes=VMEM_LIMIT),
        name="retention")(lg, x, x, x, x, cos, sin, gn)


def _na_kernel(q_ref, k_ref, v_ref, gq_ref, gk_ref, bias_ref, o_ref, qn_ref, kn_ref, *, rows):
    scale = HEAD_DIM ** -0.5
    qn_ref[...] = (_head_rms2(q_ref[0], gq_ref[...]) * scale).astype(BF16)
    kn_ref[...] = _head_rms2(k_ref[0], gk_ref[...]).astype(BF16)
    lane = lax.broadcasted_iota(jnp.int32, (GRID_W, LANES), 1)
    first = lane < HEAD_DIM
    half_rows = NA_ROWS // 2

    def body(r, c):
        r0 = jnp.clip(r - half_rows, 0, rows - NA_ROWS)
        pat = jnp.where(r < half_rows, r,
                        jnp.where(r > rows - half_rows, r - (rows - NA_ROWS), half_rows))
        q2 = qn_ref[pl.ds(pl.multiple_of(r * GRID_W, GRID_W), GRID_W), :]
        kstart = pl.multiple_of(r0 * GRID_W, GRID_W)
        k2 = kn_ref[pl.ds(kstart, NA_ROWS * GRID_W), :]
        v2 = v_ref[0, pl.ds(kstart, NA_ROWS * GRID_W), :].astype(BF16)
        outs = []
        for hh in range(2):
            qm = jnp.where(first if hh == 0 else ~first, q2, jnp.zeros_like(q2))
            s = lax.dot_general(qm, k2, (((1,), (1,)), ((), ())), preferred_element_type=F32)
            s = s + bias_ref[hh, pat]
            m = jnp.max(s, axis=-1, keepdims=True)
            e = jnp.exp(s - m)
            l = jnp.sum(e, axis=-1, keepdims=True)
            outs.append(jnp.dot(e.astype(BF16), v2, preferred_element_type=F32) / l)
        o_ref[0, pl.ds(pl.multiple_of(r * GRID_W, GRID_W), GRID_W), :] = jnp.where(first, outs[0], outs[1])
        return c

    lax.fori_loop(0, rows, body, 0)


def _na_bias_table(rpb, rows):
    kh = NA_ROWS
    c = np.arange(GRID_W)
    c0 = np.clip(c - NA_COLS // 2, 0, GRID_W - NA_COLS)
    col_valid = (c[None, :] >= c0[:, None]) & (c[None, :] < c0[:, None] + NA_COLS)
    col_off = np.clip(c[None, :] - c[:, None] + NA_COLS - 1, 0, 2 * NA_COLS - 2)
    pat_rows = list(range(kh // 2)) + [kh // 2] + list(range(rows - kh // 2 + 1, rows))
    row_sel = np.zeros((len(pat_rows), kh, 2 * NA_ROWS - 1), np.float32)
    for p, r in enumerate(pat_rows):
        r0 = min(max(r - kh // 2, 0), rows - kh)
        row_sel[p, np.arange(kh), r0 + np.arange(kh) - r + NA_ROWS - 1] = 1.0
    col_sel = (col_off[None] == np.arange(2 * NA_COLS - 1)[:, None, None]).astype(np.float32)
    b = jnp.einsum('pir,hrc,cqk->hpqik', row_sel, rpb.astype(F32), col_sel, precision=lax.Precision.HIGHEST)
    b = jnp.where(col_valid[None, None, :, None, :], b, NEG)
    return b.reshape(rpb.shape[0], len(pat_rows), GRID_W, kh * GRID_W)


def _neighbourhood_attention(slabs, gq, gk, rpb, B, S):
    HP = HEAD_PAIRS
    rows = S // GRID_W
    assert rows >= NA_ROWS
    n_pat = NA_ROWS
    bias = _na_bias_table(rpb, rows).reshape(HP, 2, n_pat, GRID_W, NA_ROWS * GRID_W)
    g2q = jnp.tile(gq.astype(F32), 2).reshape(1, LANES)
    g2k = jnp.tile(gk.astype(F32), 2).reshape(1, LANES)
    x = slabs.reshape(slabs.shape[0], B, S, LANES)
    spec = lambda g: pl.BlockSpec((None, 1, S, LANES), lambda h, b, g=g: (g * HP + h, b, 0, 0))
    vec = pl.BlockSpec((1, LANES), lambda h, b: (0, 0))
    return pl.pallas_call(
        functools.partial(_na_kernel, rows=rows), grid=(HP, B),
        in_specs=[spec(0), spec(1), spec(2), vec, vec,
                  pl.BlockSpec((None, 2, n_pat, GRID_W, NA_ROWS * GRID_W), lambda h, b: (h, 0, 0, 0, 0))],
        out_specs=pl.BlockSpec((None, 1, S, LANES), lambda h, b: (h, b, 0, 0)),
        out_shape=jax.ShapeDtypeStruct((HP, B, S, LANES), F32),
        scratch_shapes=[pltpu.VMEM((S, LANES), BF16), pltpu.VMEM((S, LANES), BF16)],
        compiler_params=pltpu.CompilerParams(dimension_semantics=("arbitrary", "arbitrary"),
                                             vmem_limit_bytes=VMEM_LIMIT),
        name="nbr_attention")(x, x, x, g2q, g2k, bias)


def _dil_kernel(q_ref, k_ref, v_ref, gq_ref, gk_ref, cos_ref, sin_ref, o_ref,
                qn_ref, kn_ref, qd_ref, kd_ref, vd_ref, od_ref, ld_ref, os_ref, ls_ref):
    S = q_ref.shape[1]
    BLK = DIL_BLK
    BQ = DIL_BQ
    NK = BQ + 2 * BLK
    cos = cos_ref[...]
    sin = sin_ref[...]
    qn_ref[...] = _rope2(_head_rms2(q_ref[0], gq_ref[...]), cos, sin) * HEAD_DIM ** -0.5
    kn_ref[...] = _rope2(_head_rms2(k_ref[0], gk_ref[...]), cos, sin)
    first = lax.broadcasted_iota(jnp.int32, (BQ, LANES), 1) < HEAD_DIM
    qi = lax.broadcasted_iota(jnp.int32, (BQ, NK), 0)
    kc = lax.broadcasted_iota(jnp.int32, (BQ, NK), 1) - BLK
    band = jnp.abs(kc - qi) <= BLK
    for bi, (window, dil) in enumerate(DIL_BRANCHES):
        assert window // (2 * dil) == BLK
        Ls = S // dil
        shift = Ls.bit_length() - 1
        assert 1 << shift == Ls and Ls >= 2 * BLK
        for r in range(dil):
            sl = pl.ds(r, Ls, stride=dil) if dil > 1 else pl.ds(0, Ls)
            qd_ref[pl.ds(r * Ls, Ls), :] = qn_ref[sl, :].astype(BF16)
            kd_ref[pl.ds(r * Ls, Ls), :] = kn_ref[sl, :].astype(BF16)
            vd_ref[pl.ds(r * Ls, Ls), :] = v_ref.at[0][sl, :].astype(BF16)

        def body(fb, c):
            row0 = pl.multiple_of(fb * BQ, BQ)
            prev0 = pl.multiple_of(jnp.maximum(row0 - BLK, 0), BLK)
            next0 = pl.multiple_of(jnp.minimum(row0 + BQ, S - BLK), BLK)
            kk = jnp.concatenate([kd_ref[pl.ds(prev0, BLK), :], kd_ref[pl.ds(row0, BQ), :],
                                  kd_ref[pl.ds(next0, BLK), :]], axis=0)
            vv = jnp.concatenate([vd_ref[pl.ds(prev0, BLK), :], vd_ref[pl.ds(row0, BQ), :],
                                  vd_ref[pl.ds(next0, BLK), :]], axis=0)
            valid = band & (((row0 + kc) >> shift) == ((row0 + qi) >> shift))
            q2 = qd_ref[pl.ds(row0, BQ), :]
            outs, lses = [], []
            for hh in range(2):
                qm = jnp.where(first if hh == 0 else ~first, q2, jnp.zeros_like(q2))
                s = lax.dot_general(qm, kk, (((1,), (1,)), ((), ())), preferred_element_type=F32)
                s = jnp.where(valid, s, NEG)
                m = jnp.max(s, axis=-1, keepdims=True)
                p = jnp.exp(s - m)
                den = jnp.sum(p, axis=-1, keepdims=True)
                outs.append(jnp.dot(p.astype(BF16), vv, preferred_element_type=F32) / den)
                lses.append(m + jnp.log(den))
            od_ref[pl.ds(row0, BQ), :] = jnp.where(first, outs[0], outs[1])
            ld_ref[pl.ds(row0, BQ), :] = jnp.where(first, lses[0], lses[1])
            return c

        lax.fori_loop(0, S // BQ, body, 0)
        for r in range(dil):
            sl = pl.ds(r, Ls, stride=dil) if dil > 1 else pl.ds(0, Ls)
            os_ref.at[bi][sl, :] = od_ref[pl.ds(r * Ls, Ls), :]
            ls_ref.at[bi][sl, :] = ld_ref[pl.ds(r * Ls, Ls), :]
    l0, l1, l2 = ls_ref[0], ls_ref[1], ls_ref[2]
    mx = jnp.maximum(jnp.maximum(l0, l1), l2)
    w0, w1, w2 = jnp.exp(l0 - mx), jnp.exp(l1 - mx), jnp.exp(l2 - mx)
    o_ref[0] = (w0 * os_ref[0] + w1 * os_ref[1] + w2 * os_ref[2]) / (w0 + w1 + w2)


def _dilated_attention(slabs, first_slab, gq, gk, cos, sin, B, S):
    HP = HEAD_PAIRS
    x = slabs.reshape(slabs.shape[0], B, S, LANES)
    g2q = jnp.tile(gq.astype(F32), 2).reshape(1, LANES)
    g2k = jnp.tile(gk.astype(F32), 2).reshape(1, LANES)
    spec = lambda g: pl.BlockSpec((None, 1, S, LANES), lambda h, b, g=g: (first_slab + g * HP + h, b, 0, 0))
    tab = pl.BlockSpec((S, LANES), lambda h, b: (0, 0))
    vec = pl.BlockSpec((1, LANES), lambda h, b: (0, 0))
    return pl.pallas_call(
        _dil_kernel, grid=(HP, B),
        in_specs=[spec(0), spec(1), spec(2), vec, vec, tab, tab],
        out_specs=pl.BlockSpec((None, 1, S, LANES), lambda h, b: (h, b, 0, 0)),
        out_shape=jax.ShapeDtypeStruct((HP, B, S, LANES), F32),
        scratch_shapes=[pltpu.VMEM((S, LANES), F32), pltpu.VMEM((S, LANES), F32),
                        pltpu.VMEM((S, LANES), BF16), pltpu.VMEM((S, LANES), BF16),
                        pltpu.VMEM((S, LANES), BF16),
                        pltpu.VMEM((S, LANES), F32), pltpu.VMEM((S, LANES), F32),
                        pltpu.VMEM((len(DIL_BRANCHES), S, LANES), F32),
                        pltpu.VMEM((len(DIL_BRANCHES), S, LANES), F32)],
        compiler_params=pltpu.CompilerParams(dimension_semantics=("arbitrary", "arbitrary"),
                                             vmem_limit_bytes=VMEM_LIMIT),
        name="dilated_attention")(x, x, x, g2q, g2k, cos, sin)


def _stair_tables():
    a_idx = [0] * 16 + [a for a in range(1, 8) for _ in range(8)] + list(range(8, 16))
    b_idx = list(range(16)) + [b for _ in range(1, 8) for b in range(8)] + [0] * 8
    a_idx = np.array(a_idx)
    b_idx = np.array(b_idx)
    valid = (a_idx + 1) * (b_idx + 1) <= PEER_TOPK
    return a_idx, b_idx, valid


_ST_A, _ST_B, _ST_VALID = _stair_tables()
N_CAND = len(_ST_A)


def _top16_rows(sc, iota):
    vals, idxs = [], []
    big = float(sc.shape[0])
    for r in range(PEER_TOPK):
        m = jnp.max(sc, axis=0, keepdims=True)
        idx = jnp.min(jnp.where(sc == m, iota, big), axis=0, keepdims=True)
        vals.append(m)
        idxs.append(idx)
        if r + 1 < PEER_TOPK:
            sc = jnp.where(iota == idx, NEG_INF, sc)
    return jnp.concatenate(vals, axis=0), jnp.concatenate(idxs, axis=0)


def _route_kernel(xn_ref, wqT_ref, keys_ref, flat_ref, e_ref, g_ref, qT_ref, eT_ref, gT_ref):
    Tb = xn_ref.shape[0]
    xb = xn_ref[...].astype(BF16)
    qT_ref[...] = lax.dot_general(wqT_ref[...], xb, (((1,), (1,)), ((), ())), preferred_element_type=F32)
    iota = lax.broadcasted_iota(jnp.int32, (PEER_KEYS, Tb), 0).astype(F32)
    flat = jnp.broadcast_to(flat_ref[...], (N_CAND, Tb))
    for h in range(PEER_HEADS):
        s, ix = [], []
        for p in range(2):
            r0 = (h * 2 + p) * (PEER_DK // 2)
            sc = jnp.dot(keys_ref[h * 2 + p], qT_ref[pl.ds(r0, PEER_DK // 2), :].astype(BF16),
                         preferred_element_type=F32)
            sp, ip = _top16_rows(sc, iota)
            s.append(sp)
            ix.append(ip)
        s0, s1 = s
        i0, i1 = ix[0] * float(PEER_KEYS), ix[1]
        parts_s = [s0[0:1] + s1]
        parts_e = [i0[0:1] + i1]
        for a in range(1, 8):
            parts_s.append(s0[a:a + 1] + s1[0:8])
            parts_e.append(i0[a:a + 1] + i1[0:8])
        parts_s.append(s0[8:16] + s1[0:1])
        parts_e.append(i0[8:16] + i1[0:1])
        cand = jnp.where(flat < 256.0, jnp.concatenate(parts_s, axis=0), NEG_INF)
        eid = jnp.concatenate(parts_e, axis=0)
        gs, es = [], []
        for r in range(PEER_TOPK):
            m = jnp.max(cand, axis=0, keepdims=True)
            fsel = jnp.min(jnp.where(cand == m, flat, float(1 << 20)), axis=0, keepdims=True)
            hit = flat == fsel
            es.append(jnp.max(jnp.where(hit, eid, -1.0), axis=0, keepdims=True))
            gs.append(m)
            if r + 1 < PEER_TOPK:
                cand = jnp.where(hit, NEG_INF, cand)
        g = jnp.concatenate(gs, axis=0)
        ex = jnp.exp(g - g[0:1])
        eT_ref[pl.ds(h * PEER_TOPK, PEER_TOPK), :] = jnp.concatenate(es, axis=0).astype(jnp.int32)
        gT_ref[pl.ds(h * PEER_TOPK, PEER_TOPK), :] = ex / jnp.sum(ex, axis=0, keepdims=True)
    e_ref[...] = eT_ref[...].T
    g_ref[...] = gT_ref[...].T


def _peer_route(xn, w_q, sub_keys):
    T = xn.shape[0]
    Tb = ROUTE_TOK
    wqT = w_q.T.astype(BF16)
    keys = sub_keys.reshape(PEER_HEADS * 2, PEER_KEYS, PEER_DK // 2).astype(BF16)
    flat = np.where(_ST_VALID, _ST_A * 16 + _ST_B, 1 << 16).astype(np.float32).reshape(N_CAND, 1)
    return pl.pallas_call(
        _route_kernel, grid=(T // Tb,),
        in_specs=[pl.BlockSpec((Tb, D_MODEL), lambda i: (i, 0)),
                  pl.BlockSpec((D_MODEL, D_MODEL), lambda i: (0, 0)),
                  pl.BlockSpec((PEER_HEADS * 2, PEER_KEYS, PEER_DK // 2), lambda i: (0, 0, 0)),
                  pl.BlockSpec((N_CAND, 1), lambda i: (0, 0))],
        out_specs=[pl.BlockSpec((Tb, PEER_SEL), lambda i: (i, 0)),
                   pl.BlockSpec((Tb, PEER_SEL), lambda i: (i, 0))],
        out_shape=[jax.ShapeDtypeStruct((T, PEER_SEL), jnp.int32),
                   jax.ShapeDtypeStruct((T, PEER_SEL), F32)],
        scratch_shapes=[pltpu.VMEM((D_MODEL, Tb), F32), pltpu.VMEM((PEER_SEL, Tb), jnp.int32),
                        pltpu.VMEM((PEER_SEL, Tb), F32)],
        compiler_params=pltpu.CompilerParams(dimension_semantics=("arbitrary",),
                                             vmem_limit_bytes=VMEM_LIMIT),
        name="peer_route")(xn, wqT, keys, jnp.asarray(flat))


def _peer_eval_kernel(idx_cur, idx_nxt, x_ref, r_ref, g_ref, uv_hbm, o_ref, buf, sem):
    i = pl.program_id(0)
    n = pl.num_programs(0)
    G = PEER_TOK
    NS = PEER_SLOTS
    assert PEER_AHEAD == NS - 1

    def issue_rows(idx_ref, row, slot, g, k0, k1):
        for k in range(k0, k1):
            e = idx_ref[row, k]
            pltpu.make_async_copy(uv_hbm.at[e], buf.at[slot, g, k // SUBLANES, :, k % SUBLANES, :],
                                  sem.at[slot]).start(priority=k % DMA_THREADS)

    def wait_slot(slot):
        pltpu.make_async_copy(buf.at[slot], buf.at[slot], sem.at[slot]).wait()

    def compute_tok(slot, g, row, gates_t, issue):
        per = PEER_SEL // SUBLANES
        xg = x_ref[row]
        acc = buf[slot, g, :, 0].reshape(PEER_SEL, LANES) * xg[0:1, :]
        for c in range(1, SUBLANES):
            acc = acc + buf[slot, g, :, c].reshape(PEER_SEL, LANES) * xg[c:c + 1, :]
        h = jnp.sum(acc, axis=-1, keepdims=True)
        a = jax.nn.gelu(h) * gates_t[:, row:row + 1]
        ab = jnp.broadcast_to(a, (PEER_SEL, LANES))
        rows = []
        for c in range(SUBLANES):
            issue(c * per, (c + 1) * per)
            rows.append(jnp.sum(ab * buf[slot, g, :, SUBLANES + c].reshape(PEER_SEL, LANES),
                                axis=0, keepdims=True))
        o_ref[row] = r_ref[row] + jnp.concatenate(rows, axis=0)

    @pl.when(i == 0)
    def _():
        for j in range(PEER_AHEAD):
            for g in range(G):
                issue_rows(idx_cur, j * G + g, j, g, 0, PEER_SEL)

    gates_t = g_ref[...].T
    for j in range(NS):
        wait_slot(j)
        ahead = j + PEER_AHEAD
        for g in range(G):
            if ahead < NS:
                src, row, slot = idx_cur, ahead * G + g, ahead
            else:
                src, row, slot = idx_nxt, (ahead - NS) * G + g, ahead - NS
            compute_tok(j, g, j * G + g, gates_t,
                        functools.partial(issue_rows, src, row, slot, g))

    @pl.when(i == n - 1)
    def _():
        for j in range(PEER_AHEAD):
            wait_slot(j)


def _peer_eval(xt, res, experts, gates, uv):
    T = xt.shape[0]
    G = PEER_TOK
    NT = PEER_SLOTS * G
    n = T // NT
    x3 = xt.reshape(T, SUBLANES, LANES)
    r3 = res.reshape(T, SUBLANES, LANES)
    tok = pl.BlockSpec((NT, SUBLANES, LANES), lambda i: (i, 0, 0))
    out = pl.pallas_call(
        _peer_eval_kernel,
        grid=(n,),
        in_specs=[
            pl.BlockSpec((NT, PEER_SEL), lambda i: (i, 0), memory_space=pltpu.SMEM),
            pl.BlockSpec((NT, PEER_SEL), lambda i: (jnp.minimum(i + 1, n - 1), 0),
                         memory_space=pltpu.SMEM),
            tok,
            tok,
            pl.BlockSpec((NT, PEER_SEL), lambda i: (i, 0)),
            pl.BlockSpec(memory_space=pl.ANY),
        ],
        out_specs=tok,
        out_shape=jax.ShapeDtypeStruct((T, SUBLANES, LANES), F32),
        scratch_shapes=[
            pltpu.VMEM((PEER_SLOTS, G, 2 * SUBLANES, 2 * SUBLANES, SUBLANES, LANES), F32),
            pltpu.SemaphoreType.DMA((PEER_SLOTS,)),
        ],
        compiler_params=pltpu.CompilerParams(
            dimension_semantics=("arbitrary",),
            vmem_limit_bytes=VMEM_LIMIT),
        name="peer_eval",
    )(experts, experts, x3, r3, gates, uv)
    return out.reshape(T, D_MODEL)


def _peer(xn, res, w_q, sub_keys, u, v):
    E = u.shape[0]
    experts, gates = _peer_route(xn, w_q, sub_keys)
    uv = jnp.concatenate([u.astype(F32).reshape(E, SUBLANES, LANES),
                          v.astype(F32).reshape(E, SUBLANES, LANES)], axis=1)
    return _peer_eval(xn, res, experts, gates, uv)


def kernel(x, mix_norm_gain, ffn_norm_gain, ev_w_in, ev_w_out, hy_conv_w, hy_conv_b, hy_fw1, hy_fb1, hy_fw2, hy_fb2, hy_fw3, hy_fb3, hy_fw_out, hy_freq, hy_skip, ret_log_decay, ret_gn_gain, od_w_in, od_w_out, na_qk_gain, na_rpb, dil_qk_gain, peer_w_q, peer_sub_keys, peer_u, peer_v):
    B, S, D = x.shape
    T = B * S
    depth = mix_norm_gain.shape[0]
    cos, sin = _rope_tables(S)
    emb, window = _filter_consts(S)
    dr, di, wk = _dft_tables(S)
    xt = x.reshape(T, D).astype(F32)
    for layer in range(depth):
        if layer % 2 == 0:
            e = layer // 2
            slabs = _norm_in_proj(xt, mix_norm_gain[layer], ev_w_in[e])
            hr, hi = _hyena_filter_spectrum(hy_fw1[e], hy_fb1[e], hy_fw2[e], hy_fb2[e], hy_fw3[e], hy_fb3[e],
                                            hy_fw_out[e], hy_freq[e], emb, window, dr, di)
            a_out = _hyena_conv(slabs, hy_conv_w[e], hy_conv_b[e], hy_skip[e], hr, hi, dr, di, wk, B, S)
            b_out = _retention(slabs, 3 * HEAD_PAIRS, ret_log_decay[e], ret_gn_gain[e], cos, sin, B, S)
            w_out = ev_w_out[e]
        else:
            o = layer // 2
            slabs = _norm_in_proj(xt, mix_norm_gain[layer], od_w_in[o])
            a_out = _neighbourhood_attention(slabs, na_qk_gain[o, 0], na_qk_gain[o, 1], na_rpb[o], B, S)
            b_out = _dilated_attention(slabs, 3 * HEAD_PAIRS, dil_qk_gain[o, 0], dil_qk_gain[o, 1],
                                       cos, sin, B, S)
            w_out = od_w_out[o]
        xt, xn = _out_proj_residual(a_out.reshape(HEAD_PAIRS, T, LANES), b_out.reshape(HEAD_PAIRS, T, LANES),
                                    w_out, xt, ffn_norm_gain[layer])
        xt = _peer(xn, xt, peer_w_q[layer], peer_sub_keys[layer], peer_u[layer], peer_v[layer])
    return xt.reshape(B, S, D).astype(x.dtype)
```

```python
import functools
import math

import numpy as np
import jax
import jax.numpy as jnp
from jax import lax
from jax.experimental import pallas as pl
from jax.experimental.pallas import tpu as pltpu

F32 = jnp.float32
BF16 = jnp.bfloat16

D_MODEL = 1024
HEAD_DIM = 64
GROUP_WIDTH = D_MODEL // 2
EPS = 1e-6
NEG = -1e30
NEG_INF = float("-inf")
ROPE_THETA = 10000.0

HY_CHANNELS = GROUP_WIDTH
HY_ORDER = 2
HY_EMB = 33
HY_EMB_PAD = 40
HY_BANDS = (HY_EMB - 1) // 2
HY_HID = 64
HY_FAST_DECAY = 0.3
HY_SLOW_DECAY = 1.5
HY_DECAY_TARGET = 1e-2
HY_MAX_DECAY = math.log(HY_DECAY_TARGET) / HY_FAST_DECAY
HY_MIN_DECAY = math.log(HY_DECAY_TARGET) / HY_SLOW_DECAY
HY_MOD_SHIFT = 0.05

GRID_W = 64
NA_HEADS = GROUP_WIDTH // HEAD_DIM
NA_ROWS = 8
NA_COLS = 16

DIL_BRANCHES = ((128, 1), (512, 4), (2048, 16))
DIL_BLK = 64
DIL_BQ = 256

PEER_HEADS = 8
PEER_KEYS = 128
PEER_DK = 128
PEER_TOPK = 16
PEER_SEL = PEER_HEADS * PEER_TOPK

SUBLANES = 8
LANES = 128
assert D_MODEL == SUBLANES * LANES
HEAD_PAIRS = GROUP_WIDTH // LANES
PEER_TOK = 8
PEER_SLOTS = 4
PEER_AHEAD = PEER_SLOTS - 1
ROUTE_TOK = 256
DMA_THREADS = 2
MM_TOK = 256
MM_NCHUNK = 512
RET_BQ = 256
HY_CB = 128
HY_FILT_CB = 256
VMEM_LIMIT = 48 * 1024 * 1024
VMEM_LIMIT_HYENA = 56 * 1024 * 1024


def _split(x):
    hi = x.astype(BF16)
    return hi, (x - hi.astype(F32)).astype(BF16)


def _dot3(a, b):
    ah, al = _split(a)
    bh, bl = _split(b)
    d = lambda x, y: jnp.dot(x, y, preferred_element_type=F32)
    return d(ah, bh) + (d(ah, bl) + d(al, bh))


def _rms_rows(x, gain):
    return x * lax.rsqrt(jnp.mean(x * x, axis=-1, keepdims=True) + EPS) * gain


def _pair_sums(x):
    half = (lax.broadcasted_iota(jnp.int32, (LANES, LANES), 0) // HEAD_DIM ==
            lax.broadcasted_iota(jnp.int32, (LANES, LANES), 1) // HEAD_DIM).astype(BF16)
    hi, lo = _split(x)
    return (jnp.dot(hi, half, preferred_element_type=F32) +
            jnp.dot(lo, half, preferred_element_type=F32))


def _head_rms2(t, gain2):
    ms = _pair_sums(t * t) * (1.0 / HEAD_DIM)
    return t * lax.rsqrt(ms + EPS) * gain2


def _rope_tables(S):
    lane = np.arange(LANES)
    inv = ROPE_THETA ** (-jnp.arange(0, HEAD_DIM, 2, dtype=F32) / HEAD_DIM)
    ang = jnp.arange(S, dtype=F32)[:, None] * inv[None, :]
    cos = jnp.tile(jnp.cos(ang), (1, LANES // (HEAD_DIM // 2)))
    sin = jnp.tile(jnp.sin(ang), (1, LANES // (HEAD_DIM // 2)))
    sign = np.where((lane % HEAD_DIM) < HEAD_DIM // 2, -1.0, 1.0).astype(np.float32)
    return cos, sin * sign[None, :]


def _rope2(t, cos, sin_signed):
    lane = lax.broadcasted_iota(jnp.int32, t.shape, 1)
    lower = (lane % HEAD_DIM) < HEAD_DIM // 2
    partner = jnp.where(lower, pltpu.roll(t, LANES - HEAD_DIM // 2, 1), pltpu.roll(t, HEAD_DIM // 2, 1))
    return t * cos + partner * sin_signed


def _mm_in_kernel(x_ref, g_ref, w_ref, o_ref):
    xn = _rms_rows(x_ref[...], g_ref[...]).astype(BF16)
    per = MM_NCHUNK // LANES
    for c in range(o_ref.shape[0] // per):
        y = jnp.dot(xn, w_ref[:, c * MM_NCHUNK:(c + 1) * MM_NCHUNK], preferred_element_type=F32)
        for s in range(per):
            o_ref[c * per + s] = y[:, s * LANES:(s + 1) * LANES]


def _norm_in_proj(x, gain, w):
    T, K = x.shape
    N = w.shape[1]
    assert N % MM_NCHUNK == 0
    return pl.pallas_call(
        _mm_in_kernel, grid=(T // MM_TOK,),
        in_specs=[pl.BlockSpec((MM_TOK, K), lambda i: (i, 0)),
                  pl.BlockSpec((1, K), lambda i: (0, 0)),
                  pl.BlockSpec((K, N), lambda i: (0, 0))],
        out_specs=pl.BlockSpec((N // LANES, MM_TOK, LANES), lambda i: (0, i, 0)),
        out_shape=jax.ShapeDtypeStruct((N // LANES, T, LANES), F32),
        compiler_params=pltpu.CompilerParams(dimension_semantics=("arbitrary",),
                                             vmem_limit_bytes=VMEM_LIMIT),
        name="norm_in_proj")(x, gain.astype(F32).reshape(1, K), w.astype(BF16))


def _mm_out_kernel(a_ref, b_ref, w_ref, x_ref, g_ref, xo_ref, xn_ref):
    lhs = jnp.concatenate([a_ref[s].astype(BF16) for s in range(a_ref.shape[0])] +
                          [b_ref[s].astype(BF16) for s in range(b_ref.shape[0])], axis=-1)
    xo = x_ref[...] + jnp.dot(lhs, w_ref[...], preferred_element_type=F32)
    xo_ref[...] = xo
    xn_ref[...] = _rms_rows(xo, g_ref[...])


def _out_proj_residual(a_slabs, b_slabs, w, x, gain):
    NA_, T, _ = a_slabs.shape
    NB_ = b_slabs.shape[0]
    N = w.shape[1]
    row = pl.BlockSpec((MM_TOK, N), lambda i: (i, 0))
    return pl.pallas_call(
        _mm_out_kernel, grid=(T // MM_TOK,),
        in_specs=[pl.BlockSpec((NA_, MM_TOK, LANES), lambda i: (0, i, 0)),
                  pl.BlockSpec((NB_, MM_TOK, LANES), lambda i: (0, i, 0)),
                  pl.BlockSpec(((NA_ + NB_) * LANES, N), lambda i: (0, 0)),
                  row,
                  pl.BlockSpec((1, N), lambda i: (0, 0))],
        out_specs=[row, row],
        out_shape=[jax.ShapeDtypeStruct((T, N), F32), jax.ShapeDtypeStruct((T, N), F32)],
        compiler_params=pltpu.CompilerParams(dimension_semantics=("arbitrary",),
                                             vmem_limit_bytes=VMEM_LIMIT),
        name="out_proj_residual")(a_slabs, b_slabs, w.astype(BF16), x, gain.astype(F32).reshape(1, N))


def _dft_tables(L):
    N = 2 * L
    NF = -(-(L + 1) // LANES) * LANES
    k = jnp.arange(NF, dtype=jnp.int32)[:, None]
    n = jnp.arange(L, dtype=jnp.int32)[None, :]
    ang = ((k * n) % N).astype(F32) * (2.0 * math.pi / N)
    live = k <= L
    fr = jnp.where(live, jnp.cos(ang), 0.0).astype(BF16)
    fi = jnp.where(live, -jnp.sin(ang), 0.0).astype(BF16)
    kk = np.arange(NF)
    wk = np.where((kk == 0) | (kk == L), 1.0, np.where(kk < L, 2.0, 0.0)) / N
    return fr, fi, jnp.asarray(wk.astype(np.float32).reshape(NF, 1))


def _filter_consts(L):
    t_norm = jnp.linspace(0.0, 1.0, L, dtype=F32)[:, None]
    w = 2.0 * math.pi * jnp.arange(L, dtype=F32)[:, None] / L
    f = jnp.linspace(1e-4, HY_BANDS - 1, HY_BANDS, dtype=F32)[None, :]
    emb = jnp.concatenate([t_norm, jnp.cos(f * w), -jnp.sin(f * w)], axis=-1)
    emb = jnp.pad(emb, ((0, 0), (0, HY_EMB_PAD - HY_EMB)))
    deltas = jnp.abs(jnp.linspace(HY_MIN_DECAY, HY_MAX_DECAY, HY_CHANNELS, dtype=F32))
    window = jnp.exp(-t_norm * deltas[None, :]) + HY_MOD_SHIFT
    return emb, window


def _filt_kernel(emb_ref, w1_ref, b1_ref, w2_ref, b2_ref, w3_ref, b3_ref, wo_ref, freq_ref, win_ref,
                 dr_ref, di_ref, hr_ref, hi_ref):
    fr = freq_ref[...]
    h = jnp.sin(fr * (_dot3(emb_ref[...], w1_ref[...]) + b1_ref[...]))
    h = jnp.sin(fr * (_dot3(h, w2_ref[...]) + b2_ref[...]))
    h = jnp.sin(fr * (_dot3(h, w3_ref[...]) + b3_ref[...]))
    win = win_ref[...]
    fwd = _dot3(h, wo_ref[0, 0]) * win
    bwd = _dot3(h, wo_ref[0, 1]) * win
    row = lax.broadcasted_iota(jnp.int32, bwd.shape, 0)
    bwd = jnp.where(row == 0, 0.0, bwd)
    sh, sl = _split(fwd + bwd)
    dh, dl = _split(fwd - bwd)
    d = lambda x, y: jnp.dot(x, y, preferred_element_type=F32)
    hr_ref[0] = d(dr_ref[...], sh) + d(dr_ref[...], sl)
    hi_ref[0] = d(di_ref[...], dh) + d(di_ref[...], dl)


def _hyena_filter_spectrum(fw1, fb1, fw2, fb2, fw3, fb3, fw_out, freq, emb, window, dr, di):
    L = emb.shape[0]
    NF = dr.shape[0]
    C = HY_CHANNELS
    w1 = jnp.pad(fw1.astype(F32), ((0, HY_EMB_PAD - HY_EMB), (0, 0)))
    wo = fw_out.astype(F32).reshape(HY_HID, HY_ORDER, 2, C).transpose(1, 2, 0, 3)
    vec = lambda v: v.astype(F32).reshape(1, HY_HID)
    full = lambda a: pl.BlockSpec(a.shape, lambda o, c: (0,) * a.ndim)
    args = (emb, w1, vec(fb1), fw2.astype(F32), vec(fb2), fw3.astype(F32), vec(fb3))
    once = pl.Buffered(1)
    out = pl.BlockSpec((1, NF, HY_FILT_CB), lambda o, c: (o, 0, c))
    return pl.pallas_call(
        _filt_kernel, grid=(HY_ORDER, C // HY_FILT_CB),
        in_specs=[full(a) for a in args] + [
            pl.BlockSpec((1, 2, HY_HID, HY_FILT_CB), lambda o, c: (o, 0, 0, c)),
            pl.BlockSpec((1, HY_HID), lambda o, c: (0, 0)),
            pl.BlockSpec((L, HY_FILT_CB), lambda o, c: (0, c)),
            pl.BlockSpec((NF, L), lambda o, c: (0, 0), pipeline_mode=once),
            pl.BlockSpec((NF, L), lambda o, c: (0, 0), pipeline_mode=once)],
        out_specs=[out, out],
        out_shape=[jax.ShapeDtypeStruct((HY_ORDER, NF, C), F32)] * 2,
        compiler_params=pltpu.CompilerParams(dimension_semantics=("arbitrary", "arbitrary"),
                                             vmem_limit_bytes=VMEM_LIMIT_HYENA),
        name="hyena_filter")(*args, wo, vec(freq), window, dr, di)


def _conv_kernel(v_ref, x1_ref, x2_ref, cw_ref, cb_ref, skip_ref, hr_ref, hi_ref, dr_ref, di_ref, wk_ref, o_ref):
    S = v_ref.shape[2]
    nsl = v_ref.shape[0]
    row = lax.broadcasted_iota(jnp.int32, (S, LANES), 0)

    def short(u_ref, g):
        cols = []
        for s in range(nsl):
            u = u_ref[s, 0]
            w = cw_ref[g, s]
            up = jnp.where(row == 0, 0.0, pltpu.roll(u, 1, 0))
            un = jnp.where(row == S - 1, 0.0, pltpu.roll(u, S - 1, 0))
            cols.append(up * w[0:1] + u * w[1:2] + un * w[2:3] + cb_ref[g, s])
        return jnp.concatenate(cols, axis=-1)

    d = lambda x, y: jnp.dot(x, y, preferred_element_type=F32)
    wk = wk_ref[...]

    def long_conv(z, o):
        zb = z.astype(BF16)
        zr = d(dr_ref[...], zb)
        zi = d(di_ref[...], zb)
        hr = hr_ref[o]
        hi = hi_ref[o]
        yr = ((zr * hr - zi * hi) * wk).T.astype(BF16)
        yi = ((zr * hi + zi * hr) * wk).T.astype(BF16)
        y = d(yr, dr_ref[...]) + d(yi, di_ref[...])
        return y.T + z * skip_ref[o]

    v = short(v_ref, 0)
    z = short(x1_ref, 1) * long_conv(v, 0)
    out = short(x2_ref, 2) * long_conv(z, 1)
    for s in range(nsl):
        o_ref[s, 0] = out[:, s * LANES:(s + 1) * LANES]


def _hyena_conv(slabs, conv_w, conv_b, skip, hr, hi, dr, di, wk, B, S):
    C = HY_CHANNELS
    NF = dr.shape[0]
    nsl = HY_CB // LANES
    nc = C // HY_CB
    x = slabs.reshape(slabs.shape[0], B, S, LANES)
    cw = conv_w.astype(F32).reshape(3, 3, C // LANES, LANES).transpose(1, 2, 0, 3)
    cb = conv_b.astype(F32).reshape(3, C // LANES, 1, LANES)
    sk = skip.astype(F32).reshape(HY_ORDER, 1, C)
    spec = lambda g: pl.BlockSpec((nsl, 1, S, LANES), lambda c, b, g=g: (g * nc + c, b, 0, 0))
    once = pl.Buffered(1)
    return pl.pallas_call(
        _conv_kernel, grid=(nc, B),
        in_specs=[spec(0), spec(1), spec(2),
                  pl.BlockSpec((3, nsl, 3, LANES), lambda c, b: (0, c, 0, 0)),
                  pl.BlockSpec((3, nsl, 1, LANES), lambda c, b: (0, c, 0, 0)),
                  pl.BlockSpec((HY_ORDER, 1, HY_CB), lambda c, b: (0, 0, c)),
                  pl.BlockSpec((HY_ORDER, NF, HY_CB), lambda c, b: (0, 0, c), pipeline_mode=once),
                  pl.BlockSpec((HY_ORDER, NF, HY_CB), lambda c, b: (0, 0, c), pipeline_mode=once),
                  pl.BlockSpec((NF, S), lambda c, b: (0, 0), pipeline_mode=once),
                  pl.BlockSpec((NF, S), lambda c, b: (0, 0), pipeline_mode=once),
                  pl.BlockSpec((NF, 1), lambda c, b: (0, 0))],
        out_specs=pl.BlockSpec((nsl, 1, S, LANES), lambda c, b: (c, b, 0, 0)),
        out_shape=jax.ShapeDtypeStruct((C // LANES, B, S, LANES), F32),
        compiler_params=pltpu.CompilerParams(dimension_semantics=("arbitrary", "arbitrary"),
                                             vmem_limit_bytes=VMEM_LIMIT_HYENA),
        name="hyena_conv")(x, x, x, cw, cb, sk, hr, hi, dr, di, wk)


def _ret_kernel(lg_ref, q_ref, k_ref, v_ref, g_ref, cos_ref, sin_ref, gn_ref, o_ref, qr_ref, kr_ref, vb_ref):
    hp = pl.program_id(0)
    S = q_ref.shape[1]
    cos = cos_ref[...]
    sin = sin_ref[...]
    qr_ref[...] = _rope2(q_ref[0], cos, sin).astype(BF16)
    kr_ref[...] = (_rope2(k_ref[0], cos, sin) * HEAD_DIM ** -0.5).astype(BF16)
    vb_ref[...] = v_ref[0].astype(BF16)
    first = lax.broadcasted_iota(jnp.int32, (RET_BQ, LANES), 1) < HEAD_DIM
    m_idx = lax.broadcasted_iota(jnp.int32, (RET_BQ, S), 1)
    n_loc = lax.broadcasted_iota(jnp.int32, (RET_BQ, S), 0)

    def body(r, c):
        row0 = pl.multiple_of(r * RET_BQ, RET_BQ)
        diff = (n_loc + row0 - m_idx).astype(F32)
        q2 = qr_ref[pl.ds(row0, RET_BQ), :]
        ys = []
        for hh in range(2):
            lgf = lg_ref[0, 2 * hp + hh]
            lgb = lg_ref[1, 2 * hp + hh]
            dec = jnp.exp(diff * jnp.where(diff >= 0.0, lgf, -lgb))
            qm = jnp.where(first if hh == 0 else ~first, q2, jnp.zeros_like(q2))
            s = lax.dot_general(qm, kr_ref[...], (((1,), (1,)), ((), ())), preferred_element_type=F32)
            ys.append(jnp.dot((s * dec).astype(BF16), vb_ref[...], preferred_element_type=F32))
        y = jnp.where(first, ys[0], ys[1])
        y = y * lax.rsqrt(_pair_sums(y * y) * (1.0 / HEAD_DIM) + EPS) * gn_ref[0]
        g = g_ref[0, pl.ds(row0, RET_BQ), :]
        o_ref[0, pl.ds(row0, RET_BQ), :] = g * jax.nn.sigmoid(g) * y
        return c

    lax.fori_loop(0, S // RET_BQ, body, 0)


def _retention(slabs, first_slab, log_decay_param, gn_gain, cos, sin, B, S):
    HP = HEAD_PAIRS
    x = slabs.reshape(slabs.shape[0], B, S, LANES)
    lg = -jnp.exp(log_decay_param.astype(F32))
    gn = gn_gain.astype(F32).reshape(HP, 1, LANES)
    spec = lambda g: pl.BlockSpec((None, 1, S, LANES), lambda h, b, g=g: (first_slab + g * HP + h, b, 0, 0))
    tab = pl.BlockSpec((S, LANES), lambda h, b: (0, 0))
    return pl.pallas_call(
        _ret_kernel, grid=(HP, B),
        in_specs=[pl.BlockSpec(memory_space=pltpu.SMEM), spec(0), spec(1), spec(2), spec(3), tab, tab,
                  pl.BlockSpec((None, 1, LANES), lambda h, b: (h, 0, 0))],
        out_specs=pl.BlockSpec((None, 1, S, LANES), lambda h, b: (h, b, 0, 0)),
        out_shape=jax.ShapeDtypeStruct((HP, B, S, LANES), F32),
        scratch_shapes=[pltpu.VMEM((S, LANES), BF16)] * 3,
        compiler_params=pltpu.CompilerParams(dimension_semantics=("arbitrary", "arbitrary"),
                                             vmem_limit_bytes=VMEM_LIMIT),
        name="retention")(lg, x, x, x, x, cos, sin, gn)


def _na_kernel(q_ref, k_ref, v_ref, gq_ref, gk_ref, bias_ref, o_ref, qn_ref, kn_ref, *, rows):
    scale = HEAD_DIM ** -0.5
    qn_ref[...] = (_head_rms2(q_ref[0], gq_ref[...]) * scale).astype(BF16)
    kn_ref[...] = _head_rms2(k_ref[0], gk_ref[...]).astype(BF16)
    lane = lax.broadcasted_iota(jnp.int32, (GRID_W, LANES), 1)
    first = lane < HEAD_DIM
    half_rows = NA_ROWS // 2

    def body(r, c):
        r0 = jnp.clip(r - half_rows, 0, rows - NA_ROWS)
        pat = jnp.where(r < half_rows, r,
                        jnp.where(r > rows - half_rows, r - (rows - NA_ROWS), half_rows))
        q2 = qn_ref[pl.ds(pl.multiple_of(r * GRID_W, GRID_W), GRID_W), :]
        kstart = pl.multiple_of(r0 * GRID_W, GRID_W)
        k2 = kn_ref[pl.ds(kstart, NA_ROWS * GRID_W), :]
        v2 = v_ref[0, pl.ds(kstart, NA_ROWS * GRID_W), :].astype(BF16)
        outs = []
        for hh in range(2):
            qm = jnp.where(first if hh == 0 else ~first, q2, jnp.zeros_like(q2))
            s = lax.dot_general(qm, k2, (((1,), (1,)), ((), ())), preferred_element_type=F32)
            s = s + bias_ref[hh, pat]
            m = jnp.max(s, axis=-1, keepdims=True)
            e = jnp.exp(s - m)
            l = jnp.sum(e, axis=-1, keepdims=True)
            outs.append(jnp.dot(e.astype(BF16), v2, preferred_element_type=F32) / l)
        o_ref[0, pl.ds(pl.multiple_of(r * GRID_W, GRID_W), GRID_W), :] = jnp.where(first, outs[0], outs[1])
        return c

    lax.fori_loop(0, rows, body, 0)


def _na_bias_table(rpb, rows):
    kh = NA_ROWS
    c = np.arange(GRID_W)
    c0 = np.clip(c - NA_COLS // 2, 0, GRID_W - NA_COLS)
    col_valid = (c[None, :] >= c0[:, None]) & (c[None, :] < c0[:, None] + NA_COLS)
    col_off = np.clip(c[None, :] - c[:, None] + NA_COLS - 1, 0, 2 * NA_COLS - 2)
    pat_rows = list(range(kh // 2)) + [kh // 2] + list(range(rows - kh // 2 + 1, rows))
    row_sel = np.zeros((len(pat_rows), kh, 2 * NA_ROWS - 1), np.float32)
    for p, r in enumerate(pat_rows):
        r0 = min(max(r - kh // 2, 0), rows - kh)
        row_sel[p, np.arange(kh), r0 + np.arange(kh) - r + NA_ROWS - 1] = 1.0
    col_sel = (col_off[None] == np.arange(2 * NA_COLS - 1)[:, None, None]).astype(np.float32)
    b = jnp.einsum('pir,hrc,cqk->hpqik', row_sel, rpb.astype(F32), col_sel, precision=lax.Precision.HIGHEST)
    b = jnp.where(col_valid[None, None, :, None, :], b, NEG)
    return b.reshape(rpb.shape[0], len(pat_rows), GRID_W, kh * GRID_W)


def _neighbourhood_attention(slabs, gq, gk, rpb, B, S):
    HP = HEAD_PAIRS
    rows = S // GRID_W
    assert rows >= NA_ROWS
    n_pat = NA_ROWS
    bias = _na_bias_table(rpb, rows).reshape(HP, 2, n_pat, GRID_W, NA_ROWS * GRID_W)
    g2q = jnp.tile(gq.astype(F32), 2).reshape(1, LANES)
    g2k = jnp.tile(gk.astype(F32), 2).reshape(1, LANES)
    x = slabs.reshape(slabs.shape[0], B, S, LANES)
    spec = lambda g: pl.BlockSpec((None, 1, S, LANES), lambda h, b, g=g: (g * HP + h, b, 0, 0))
    vec = pl.BlockSpec((1, LANES), lambda h, b: (0, 0))
    return pl.pallas_call(
        functools.partial(_na_kernel, rows=rows), grid=(HP, B),
        in_specs=[spec(0), spec(1), spec(2), vec, vec,
                  pl.BlockSpec((None, 2, n_pat, GRID_W, NA_ROWS * GRID_W), lambda h, b: (h, 0, 0, 0, 0))],
        out_specs=pl.BlockSpec((None, 1, S, LANES), lambda h, b: (h, b, 0, 0)),
        out_shape=jax.ShapeDtypeStruct((HP, B, S, LANES), F32),
        scratch_shapes=[pltpu.VMEM((S, LANES), BF16), pltpu.VMEM((S, LANES), BF16)],
        compiler_params=pltpu.CompilerParams(dimension_semantics=("arbitrary", "arbitrary"),
                                             vmem_limit_bytes=VMEM_LIMIT),
        name="nbr_attention")(x, x, x, g2q, g2k, bias)


def _dil_kernel(q_ref, k_ref, v_ref, gq_ref, gk_ref, cos_ref, sin_ref, o_ref,
                qn_ref, kn_ref, qd_ref, kd_ref, vd_ref, od_ref, ld_ref, os_ref, ls_ref):
    S = q_ref.shape[1]
    BLK = DIL_BLK
    BQ = DIL_BQ
    NK = BQ + 2 * BLK
    cos = cos_ref[...]
    sin = sin_ref[...]
    qn_ref[...] = _rope2(_head_rms2(q_ref[0], gq_ref[...]), cos, sin) * HEAD_DIM ** -0.5
    kn_ref[...] = _rope2(_head_rms2(k_ref[0], gk_ref[...]), cos, sin)
    first = lax.broadcasted_iota(jnp.int32, (BQ, LANES), 1) < HEAD_DIM
    qi = lax.broadcasted_iota(jnp.int32, (BQ, NK), 0)
    kc = lax.broadcasted_iota(jnp.int32, (BQ, NK), 1) - BLK
    band = jnp.abs(kc - qi) <= BLK
    for bi, (window, dil) in enumerate(DIL_BRANCHES):
        assert window // (2 * dil) == BLK
        Ls = S // dil
        shift = Ls.bit_length() - 1
        assert 1 << shift == Ls and Ls >= 2 * BLK
        for r in range(dil):
            sl = pl.ds(r, Ls, stride=dil) if dil > 1 else pl.ds(0, Ls)
            qd_ref[pl.ds(r * Ls, Ls), :] = qn_ref[sl, :].astype(BF16)
            kd_ref[pl.ds(r * Ls, Ls), :] = kn_ref[sl, :].astype(BF16)
            vd_ref[pl.ds(r * Ls, Ls), :] = v_ref.at[0][sl, :].astype(BF16)

        def body(fb, c):
            row0 = pl.multiple_of(fb * BQ, BQ)
            prev0 = pl.multiple_of(jnp.maximum(row0 - BLK, 0), BLK)
            next0 = pl.multiple_of(jnp.minimum(row0 + BQ, S - BLK), BLK)
            kk = jnp.concatenate([kd_ref[pl.ds(prev0, BLK), :], kd_ref[pl.ds(row0, BQ), :],
                                  kd_ref[pl.ds(next0, BLK), :]], axis=0)
            vv = jnp.concatenate([vd_ref[pl.ds(prev0, BLK), :], vd_ref[pl.ds(row0, BQ), :],
                                  vd_ref[pl.ds(next0, BLK), :]], axis=0)
            valid = band & (((row0 + kc) >> shift) == ((row0 + qi) >> shift))
            q2 = qd_ref[pl.ds(row0, BQ), :]
            outs, lses = [], []
            for hh in range(2):
                qm = jnp.where(first if hh == 0 else ~first, q2, jnp.zeros_like(q2))
                s = lax.dot_general(qm, kk, (((1,), (1,)), ((), ())), preferred_element_type=F32)
                s = jnp.where(valid, s, NEG)
                m = jnp.max(s, axis=-1, keepdims=True)
                p = jnp.exp(s - m)
                den = jnp.sum(p, axis=-1, keepdims=True)
                outs.append(jnp.dot(p.astype(BF16), vv, preferred_element_type=F32) / den)
                lses.append(m + jnp.log(den))
            od_ref[pl.ds(row0, BQ), :] = jnp.where(first, outs[0], outs[1])
            ld_ref[pl.ds(row0, BQ), :] = jnp.where(first, lses[0], lses[1])
            return c

        lax.fori_loop(0, S // BQ, body, 0)
        for r in range(dil):
            sl = pl.ds(r, Ls, stride=dil) if dil > 1 else pl.ds(0, Ls)
            os_ref.at[bi][sl, :] = od_ref[pl.ds(r * Ls, Ls), :]
            ls_ref.at[bi][sl, :] = ld_ref[pl.ds(r * Ls, Ls), :]
    l0, l1, l2 = ls_ref[0], ls_ref[1], ls_ref[2]
    mx = jnp.maximum(jnp.maximum(l0, l1), l2)
    w0, w1, w2 = jnp.exp(l0 - mx), jnp.exp(l1 - mx), jnp.exp(l2 - mx)
    o_ref[0] = (w0 * os_ref[0] + w1 * os_ref[1] + w2 * os_ref[2]) / (w0 + w1 + w2)


def _dilated_attention(slabs, first_slab, gq, gk, cos, sin, B, S):
    HP = HEAD_PAIRS
    x = slabs.reshape(slabs.shape[0], B, S, LANES)
    g2q = jnp.tile(gq.astype(F32), 2).reshape(1, LANES)
    g2k = jnp.tile(gk.astype(F32), 2).reshape(1, LANES)
    spec = lambda g: pl.BlockSpec((None, 1, S, LANES), lambda h, b, g=g: (first_slab + g * HP + h, b, 0, 0))
    tab = pl.BlockSpec((S, LANES), lambda h, b: (0, 0))
    vec = pl.BlockSpec((1, LANES), lambda h, b: (0, 0))
    return pl.pallas_call(
        _dil_kernel, grid=(HP, B),
        in_specs=[spec(0), spec(1), spec(2), vec, vec, tab, tab],
        out_specs=pl.BlockSpec((None, 1, S, LANES), lambda h, b: (h, b, 0, 0)),
        out_shape=jax.ShapeDtypeStruct((HP, B, S, LANES), F32),
        scratch_shapes=[pltpu.VMEM((S, LANES), F32), pltpu.VMEM((S, LANES), F32),
                        pltpu.VMEM((S, LANES), BF16), pltpu.VMEM((S, LANES), BF16),
                        pltpu.VMEM((S, LANES), BF16),
                        pltpu.VMEM((S, LANES), F32), pltpu.VMEM((S, LANES), F32),
                        pltpu.VMEM((len(DIL_BRANCHES), S, LANES), F32),
                        pltpu.VMEM((len(DIL_BRANCHES), S, LANES), F32)],
        compiler_params=pltpu.CompilerParams(dimension_semantics=("arbitrary", "arbitrary"),
                                             vmem_limit_bytes=VMEM_LIMIT),
        name="dilated_attention")(x, x, x, g2q, g2k, cos, sin)


def _stair_tables():
    a_idx = [0] * 16 + [a for a in range(1, 8) for _ in range(8)] + list(range(8, 16))
    b_idx = list(range(16)) + [b for _ in range(1, 8) for b in range(8)] + [0] * 8
    a_idx = np.array(a_idx)
    b_idx = np.array(b_idx)
    valid = (a_idx + 1) * (b_idx + 1) <= PEER_TOPK
    return a_idx, b_idx, valid


_ST_A, _ST_B, _ST_VALID = _stair_tables()
N_CAND = len(_ST_A)


def _top16_rows(sc, iota):
    vals, idxs = [], []
    big = float(sc.shape[0])
    for r in range(PEER_TOPK):
        m = jnp.max(sc, axis=0, keepdims=True)
        idx = jnp.min(jnp.where(sc == m, iota, big), axis=0, keepdims=True)
        vals.append(m)
        idxs.append(idx)
        if r + 1 < PEER_TOPK:
            sc = jnp.where(iota == idx, NEG_INF, sc)
    return jnp.concatenate(vals, axis=0), jnp.concatenate(idxs, axis=0)


def _route_kernel(xn_ref, wqT_ref, keys_ref, flat_ref, e_ref, g_ref, qT_ref, eT_ref, gT_ref):
    Tb = xn_ref.shape[0]
    xb = xn_ref[...].astype(BF16)
    qT_ref[...] = lax.dot_general(wqT_ref[...], xb, (((1,), (1,)), ((), ())), preferred_element_type=F32)
    iota = lax.broadcasted_iota(jnp.int32, (PEER_KEYS, Tb), 0).astype(F32)
    flat = jnp.broadcast_to(flat_ref[...], (N_CAND, Tb))
    for h in range(PEER_HEADS):
        s, ix = [], []
        for p in range(2):
            r0 = (h * 2 + p) * (PEER_DK // 2)
            sc = jnp.dot(keys_ref[h * 2 + p], qT_ref[pl.ds(r0, PEER_DK // 2), :].astype(BF16),
                         preferred_element_type=F32)
            sp, ip = _top16_rows(sc, iota)
            s.append(sp)
            ix.append(ip)
        s0, s1 = s
        i0, i1 = ix[0] * float(PEER_KEYS), ix[1]
        parts_s = [s0[0:1] + s1]
        parts_e = [i0[0:1] + i1]
        for a in range(1, 8):
            parts_s.append(s0[a:a + 1] + s1[0:8])
            parts_e.append(i0[a:a + 1] + i1[0:8])
        parts_s.append(s0[8:16] + s1[0:1])
        parts_e.append(i0[8:16] + i1[0:1])
        cand = jnp.where(flat < 256.0, jnp.concatenate(parts_s, axis=0), NEG_INF)
        eid = jnp.concatenate(parts_e, axis=0)
        gs, es = [], []
        for r in range(PEER_TOPK):
            m = jnp.max(cand, axis=0, keepdims=True)
            fsel = jnp.min(jnp.where(cand == m, flat, float(1 << 20)), axis=0, keepdims=True)
            hit = flat == fsel
            es.append(jnp.max(jnp.where(hit, eid, -1.0), axis=0, keepdims=True))
            gs.append(m)
            if r + 1 < PEER_TOPK:
                cand = jnp.where(hit, NEG_INF, cand)
        g = jnp.concatenate(gs, axis=0)
        ex = jnp.exp(g - g[0:1])
        eT_ref[pl.ds(h * PEER_TOPK, PEER_TOPK), :] = jnp.concatenate(es, axis=0).astype(jnp.int32)
        gT_ref[pl.ds(h * PEER_TOPK, PEER_TOPK), :] = ex / jnp.sum(ex, axis=0, keepdims=True)
    e_ref[...] = eT_ref[...].T
    g_ref[...] = gT_ref[...].T


def _peer_route(xn, w_q, sub_keys):
    T = xn.shape[0]
    Tb = ROUTE_TOK
    wqT = w_q.T.astype(BF16)
    keys = sub_keys.reshape(PEER_HEADS * 2, PEER_KEYS, PEER_DK // 2).astype(BF16)
    flat = np.where(_ST_VALID, _ST_A * 16 + _ST_B, 1 << 16).astype(np.float32).reshape(N_CAND, 1)
    return pl.pallas_call(
        _route_kernel, grid=(T // Tb,),
        in_specs=[pl.BlockSpec((Tb, D_MODEL), lambda i: (i, 0)),
                  pl.BlockSpec((D_MODEL, D_MODEL), lambda i: (0, 0)),
                  pl.BlockSpec((PEER_HEADS * 2, PEER_KEYS, PEER_DK // 2), lambda i: (0, 0, 0)),
                  pl.BlockSpec((N_CAND, 1), lambda i: (0, 0))],
        out_specs=[pl.BlockSpec((Tb, PEER_SEL), lambda i: (i, 0)),
                   pl.BlockSpec((Tb, PEER_SEL), lambda i: (i, 0))],
        out_shape=[jax.ShapeDtypeStruct((T, PEER_SEL), jnp.int32),
                   jax.ShapeDtypeStruct((T, PEER_SEL), F32)],
        scratch_shapes=[pltpu.VMEM((D_MODEL, Tb), F32), pltpu.VMEM((PEER_SEL, Tb), jnp.int32),
                        pltpu.VMEM((PEER_SEL, Tb), F32)],
        compiler_params=pltpu.CompilerParams(dimension_semantics=("arbitrary",),
                                             vmem_limit_bytes=VMEM_LIMIT),
        name="peer_route")(xn, wqT, keys, jnp.asarray(flat))


def _peer_eval_kernel(idx_cur, idx_nxt, x_ref, r_ref, g_ref, uv_hbm, o_ref, buf, sem):
    i = pl.program_id(0)
    n = pl.num_programs(0)
    G = PEER_TOK
    NS = PEER_SLOTS
    assert PEER_AHEAD == NS - 1

    def issue_rows(idx_ref, row, slot, g, k0, k1):
        for k in range(k0, k1):
            e = idx_ref[row, k]
            pltpu.make_async_copy(uv_hbm.at[e], buf.at[slot, g, k // SUBLANES, :, k % SUBLANES, :],
                                  sem.at[slot]).start(priority=k % DMA_THREADS)

    def wait_slot(slot):
        pltpu.make_async_copy(buf.at[slot], buf.at[slot], sem.at[slot]).wait()

    def compute_tok(slot, g, row, gates_t, issue):
        per = PEER_SEL // SUBLANES
        xg = x_ref[row]
        acc = buf[slot, g, :, 0].reshape(PEER_SEL, LANES) * xg[0:1, :]
        for c in range(1, SUBLANES):
            acc = acc + buf[slot, g, :, c].reshape(PEER_SEL, LANES) * xg[c:c + 1, :]
        h = jnp.sum(acc, axis=-1, keepdims=True)
        a = jax.nn.gelu(h) * gates_t[:, row:row + 1]
        ab = jnp.broadcast_to(a, (PEER_SEL, LANES))
        rows = []
        for c in range(SUBLANES):
            issue(c * per, (c + 1) * per)
            rows.append(jnp.sum(ab * buf[slot, g, :, SUBLANES + c].reshape(PEER_SEL, LANES),
                                axis=0, keepdims=True))
        o_ref[row] = r_ref[row] + jnp.concatenate(rows, axis=0)

    @pl.when(i == 0)
    def _():
        for j in range(PEER_AHEAD):
            for g in range(G):
                issue_rows(idx_cur, j * G + g, j, g, 0, PEER_SEL)

    gates_t = g_ref[...].T
    for j in range(NS):
        wait_slot(j)
        ahead = j + PEER_AHEAD
        for g in range(G):
            if ahead < NS:
                src, row, slot = idx_cur, ahead * G + g, ahead
            else:
                src, row, slot = idx_nxt, (ahead - NS) * G + g, ahead - NS
            compute_tok(j, g, j * G + g, gates_t,
                        functools.partial(issue_rows, src, row, slot, g))

    @pl.when(i == n - 1)
    def _():
        for j in range(PEER_AHEAD):
            wait_slot(j)


def _peer_eval(xt, res, experts, gates, uv):
    T = xt.shape[0]
    G = PEER_TOK
    NT = PEER_SLOTS * G
    n = T // NT
    x3 = xt.reshape(T, SUBLANES, LANES)
    r3 = res.reshape(T, SUBLANES, LANES)
    tok = pl.BlockSpec((NT, SUBLANES, LANES), lambda i: (i, 0, 0))
    out = pl.pallas_call(
        _peer_eval_kernel,
        grid=(n,),
        in_specs=[
            pl.BlockSpec((NT, PEER_SEL), lambda i: (i, 0), memory_space=pltpu.SMEM),
            pl.BlockSpec((NT, PEER_SEL), lambda i: (jnp.minimum(i + 1, n - 1), 0),
                         memory_space=pltpu.SMEM),
            tok,
            tok,
            pl.BlockSpec((NT, PEER_SEL), lambda i: (i, 0)),
            pl.BlockSpec(memory_space=pl.ANY),
        ],
        out_specs=tok,
        out_shape=jax.ShapeDtypeStruct((T, SUBLANES, LANES), F32),
        scratch_shapes=[
            pltpu.VMEM((PEER_SLOTS, G, 2 * SUBLANES, 2 * SUBLANES, SUBLANES, LANES), F32),
            pltpu.SemaphoreType.DMA((PEER_SLOTS,)),
        ],
        compiler_params=pltpu.CompilerParams(
            dimension_semantics=("arbitrary",),
            vmem_limit_bytes=VMEM_LIMIT),
        name="peer_eval",
    )(experts, experts, x3, r3, gates, uv)
    return out.reshape(T, D_MODEL)


def _peer(xn, res, w_q, sub_keys, u, v):
    E = u.shape[0]
    experts, gates = _peer_route(xn, w_q, sub_keys)
    uv = jnp.concatenate([u.astype(F32).reshape(E, SUBLANES, LANES),
                          v.astype(F32).reshape(E, SUBLANES, LANES)], axis=1)
    return _peer_eval(xn, res, experts, gates, uv)


def kernel(x, mix_norm_gain, ffn_norm_gain, ev_w_in, ev_w_out, hy_conv_w, hy_conv_b, hy_fw1, hy_fb1, hy_fw2, hy_fb2, hy_fw3, hy_fb3, hy_fw_out, hy_freq, hy_skip, ret_log_decay, ret_gn_gain, od_w_in, od_w_out, na_qk_gain, na_rpb, dil_qk_gain, peer_w_q, peer_sub_keys, peer_u, peer_v):
    B, S, D = x.shape
    T = B * S
    depth = mix_norm_gain.shape[0]
    cos, sin = _rope_tables(S)
    emb, window = _filter_consts(S)
    dr, di, wk = _dft_tables(S)
    xt = x.reshape(T, D).astype(F32)
    for layer in range(depth):
        if layer % 2 == 0:
            e = layer // 2
            slabs = _norm_in_proj(xt, mix_norm_gain[layer], ev_w_in[e])
            hr, hi = _hyena_filter_spectrum(hy_fw1[e], hy_fb1[e], hy_fw2[e], hy_fb2[e], hy_fw3[e], hy_fb3[e],
                                            hy_fw_out[e], hy_freq[e], emb, window, dr, di)
            a_out = _hyena_conv(slabs, hy_conv_w[e], hy_conv_b[e], hy_skip[e], hr, hi, dr, di, wk, B, S)
            b_out = _retention(slabs, 3 * HEAD_PAIRS, ret_log_decay[e], ret_gn_gain[e], cos, sin, B, S)
            w_out = ev_w_out[e]
        else:
            o = layer // 2
            slabs = _norm_in_proj(xt, mix_norm_gain[layer], od_w_in[o])
            a_out = _neighbourhood_attention(slabs, na_qk_gain[o, 0], na_qk_gain[o, 1], na_rpb[o], B, S)
            b_out = _dilated_attention(slabs, 3 * HEAD_PAIRS, dil_qk_gain[o, 0], dil_qk_gain[o, 1],
                                       cos, sin, B, S)
            w_out = od_w_out[o]
        xt, xn = _out_proj_residual(a_out.reshape(HEAD_PAIRS, T, LANES), b_out.reshape(HEAD_PAIRS, T, LANES),
                                    w_out, xt, ffn_norm_gain[layer])
        xt = _peer(xn, xt, peer_w_q[layer], peer_sub_keys[layer], peer_u[layer], peer_v[layer])
    return xt.reshape(B, S, D).astype(x.dtype)
```

```python
import functools
import math

import numpy as np
import jax
import jax.numpy as jnp
from jax import lax
from jax.experimental import pallas as pl
from jax.experimental.pallas import tpu as pltpu

F32 = jnp.float32
BF16 = jnp.bfloat16

D_MODEL = 1024
HEAD_DIM = 64
GROUP_WIDTH = D_MODEL // 2
EPS = 1e-6
NEG = -1e30
NEG_INF = float("-inf")
ROPE_THETA = 10000.0

HY_CHANNELS = GROUP_WIDTH
HY_ORDER = 2
HY_EMB = 33
HY_EMB_PAD = 40
HY_BANDS = (HY_EMB - 1) // 2
HY_HID = 64
HY_FAST_DECAY = 0.3
HY_SLOW_DECAY = 1.5
HY_DECAY_TARGET = 1e-2
HY_MAX_DECAY = math.log(HY_DECAY_TARGET) / HY_FAST_DECAY
HY_MIN_DECAY = math.log(HY_DECAY_TARGET) / HY_SLOW_DECAY
HY_MOD_SHIFT = 0.05

GRID_W = 64
NA_HEADS = GROUP_WIDTH // HEAD_DIM
NA_ROWS = 8
NA_COLS = 16

DIL_BRANCHES = ((128, 1), (512, 4), (2048, 16))
DIL_BLK = 64
DIL_BQ = 256

PEER_HEADS = 8
PEER_KEYS = 128
PEER_DK = 128
PEER_TOPK = 16
PEER_SEL = PEER_HEADS * PEER_TOPK

SUBLANES = 8
LANES = 128
assert D_MODEL == SUBLANES * LANES
HEAD_PAIRS = GROUP_WIDTH // LANES
PEER_TOK = 4
PEER_SLOTS = 4
PEER_AHEAD = PEER_SLOTS - 1
ROUTE_TOK = 256
DMA_THREADS = 2
MM_TOK = 256
MM_NCHUNK = 512
RET_BQ = 256
HY_CB = 128
HY_FILT_CB = 256
VMEM_LIMIT = 48 * 1024 * 1024
VMEM_LIMIT_HYENA = 56 * 1024 * 1024


def _split(x):
    hi = x.astype(BF16)
    return hi, (x - hi.astype(F32)).astype(BF16)


def _dot3(a, b):
    ah, al = _split(a)
    bh, bl = _split(b)
    d = lambda x, y: jnp.dot(x, y, preferred_element_type=F32)
    return d(ah, bh) + (d(ah, bl) + d(al, bh))


def _rms_rows(x, gain):
    return x * lax.rsqrt(jnp.mean(x * x, axis=-1, keepdims=True) + EPS) * gain


def _pair_sums(x):
    half = (lax.broadcasted_iota(jnp.int32, (LANES, LANES), 0) // HEAD_DIM ==
            lax.broadcasted_iota(jnp.int32, (LANES, LANES), 1) // HEAD_DIM).astype(BF16)
    hi, lo = _split(x)
    return (jnp.dot(hi, half, preferred_element_type=F32) +
            jnp.dot(lo, half, preferred_element_type=F32))


def _head_rms2(t, gain2):
    ms = _pair_sums(t * t) * (1.0 / HEAD_DIM)
    return t * lax.rsqrt(ms + EPS) * gain2


def _rope_tables(S):
    lane = np.arange(LANES)
    inv = ROPE_THETA ** (-jnp.arange(0, HEAD_DIM, 2, dtype=F32) / HEAD_DIM)
    ang = jnp.arange(S, dtype=F32)[:, None] * inv[None, :]
    cos = jnp.tile(jnp.cos(ang), (1, LANES // (HEAD_DIM // 2)))
    sin = jnp.tile(jnp.sin(ang), (1, LANES // (HEAD_DIM // 2)))
    sign = np.where((lane % HEAD_DIM) < HEAD_DIM // 2, -1.0, 1.0).astype(np.float32)
    return cos, sin * sign[None, :]


def _rope2(t, cos, sin_signed):
    lane = lax.broadcasted_iota(jnp.int32, t.shape, 1)
    lower = (lane % HEAD_DIM) < HEAD_DIM // 2
    partner = jnp.where(lower, pltpu.roll(t, LANES - HEAD_DIM // 2, 1), pltpu.roll(t, HEAD_DIM // 2, 1))
    return t * cos + partner * sin_signed


def _mm_in_kernel(x_ref, g_ref, w_ref, o_ref):
    xn = _rms_rows(x_ref[...], g_ref[...]).astype(BF16)
    per = MM_NCHUNK // LANES
    for c in range(o_ref.shape[0] // per):
        y = jnp.dot(xn, w_ref[:, c * MM_NCHUNK:(c + 1) * MM_NCHUNK], preferred_element_type=F32)
        for s in range(per):
            o_ref[c * per + s] = y[:, s * LANES:(s + 1) * LANES]


def _norm_in_proj(x, gain, w):
    T, K = x.shape
    N = w.shape[1]
    assert N % MM_NCHUNK == 0
    return pl.pallas_call(
        _mm_in_kernel, grid=(T // MM_TOK,),
        in_specs=[pl.BlockSpec((MM_TOK, K), lambda i: (i, 0)),
                  pl.BlockSpec((1, K), lambda i: (0, 0)),
                  pl.BlockSpec((K, N), lambda i: (0, 0))],
        out_specs=pl.BlockSpec((N // LANES, MM_TOK, LANES), lambda i: (0, i, 0)),
        out_shape=jax.ShapeDtypeStruct((N // LANES, T, LANES), F32),
        compiler_params=pltpu.CompilerParams(dimension_semantics=("arbitrary",),
                                             vmem_limit_bytes=VMEM_LIMIT),
        name="norm_in_proj")(x, gain.astype(F32).reshape(1, K), w.astype(BF16))


def _mm_out_kernel(a_ref, b_ref, w_ref, x_ref, g_ref, xo_ref, xn_ref):
    lhs = jnp.concatenate([a_ref[s].astype(BF16) for s in range(a_ref.shape[0])] +
                          [b_ref[s].astype(BF16) for s in range(b_ref.shape[0])], axis=-1)
    xo = x_ref[...] + jnp.dot(lhs, w_ref[...], preferred_element_type=F32)
    xo_ref[...] = xo
    xn_ref[...] = _rms_rows(xo, g_ref[...])


def _out_proj_residual(a_slabs, b_slabs, w, x, gain):
    NA_, T, _ = a_slabs.shape
    NB_ = b_slabs.shape[0]
    N = w.shape[1]
    row = pl.BlockSpec((MM_TOK, N), lambda i: (i, 0))
    return pl.pallas_call(
        _mm_out_kernel, grid=(T // MM_TOK,),
        in_specs=[pl.BlockSpec((NA_, MM_TOK, LANES), lambda i: (0, i, 0)),
                  pl.BlockSpec((NB_, MM_TOK, LANES), lambda i: (0, i, 0)),
                  pl.BlockSpec(((NA_ + NB_) * LANES, N), lambda i: (0, 0)),
                  row,
                  pl.BlockSpec((1, N), lambda i: (0, 0))],
        out_specs=[row, row],
        out_shape=[jax.ShapeDtypeStruct((T, N), F32), jax.ShapeDtypeStruct((T, N), F32)],
        compiler_params=pltpu.CompilerParams(dimension_semantics=("arbitrary",),
                                             vmem_limit_bytes=VMEM_LIMIT),
        name="out_proj_residual")(a_slabs, b_slabs, w.astype(BF16), x, gain.astype(F32).reshape(1, N))


def _dft_tables(L):
    N = 2 * L
    NF = -(-(L + 1) // LANES) * LANES
    k = jnp.arange(NF, dtype=jnp.int32)[:, None]
    n = jnp.arange(L, dtype=jnp.int32)[None, :]
    ang = ((k * n) % N).astype(F32) * (2.0 * math.pi / N)
    live = k <= L
    fr = jnp.where(live, jnp.cos(ang), 0.0).astype(BF16)
    fi = jnp.where(live, -jnp.sin(ang), 0.0).astype(BF16)
    kk = np.arange(NF)
    wk = np.where((kk == 0) | (kk == L), 1.0, np.where(kk < L, 2.0, 0.0)) / N
    return fr, fi, jnp.asarray(wk.astype(np.float32).reshape(NF, 1))


def _filter_consts(L):
    t_norm = jnp.linspace(0.0, 1.0, L, dtype=F32)[:, None]
    w = 2.0 * math.pi * jnp.arange(L, dtype=F32)[:, None] / L
    f = jnp.linspace(1e-4, HY_BANDS - 1, HY_BANDS, dtype=F32)[None, :]
    emb = jnp.concatenate([t_norm, jnp.cos(f * w), -jnp.sin(f * w)], axis=-1)
    emb = jnp.pad(emb, ((0, 0), (0, HY_EMB_PAD - HY_EMB)))
    deltas = jnp.abs(jnp.linspace(HY_MIN_DECAY, HY_MAX_DECAY, HY_CHANNELS, dtype=F32))
    window = jnp.exp(-t_norm * deltas[None, :]) + HY_MOD_SHIFT
    return emb, window


def _filt_kernel(emb_ref, w1_ref, b1_ref, w2_ref, b2_ref, w3_ref, b3_ref, wo_ref, freq_ref, win_ref,
                 dr_ref, di_ref, hr_ref, hi_ref):
    fr = freq_ref[...]
    h = jnp.sin(fr * (_dot3(emb_ref[...], w1_ref[...]) + b1_ref[...]))
    h = jnp.sin(fr * (_dot3(h, w2_ref[...]) + b2_ref[...]))
    h = jnp.sin(fr * (_dot3(h, w3_ref[...]) + b3_ref[...]))
    win = win_ref[...]
    fwd = _dot3(h, wo_ref[0, 0]) * win
    bwd = _dot3(h, wo_ref[0, 1]) * win
    row = lax.broadcasted_iota(jnp.int32, bwd.shape, 0)
    bwd = jnp.where(row == 0, 0.0, bwd)
    sh, sl = _split(fwd + bwd)
    dh, dl = _split(fwd - bwd)
    d = lambda x, y: jnp.dot(x, y, preferred_element_type=F32)
    hr_ref[0] = d(dr_ref[...], sh) + d(dr_ref[...], sl)
    hi_ref[0] = d(di_ref[...], dh) + d(di_ref[...], dl)


def _hyena_filter_spectrum(fw1, fb1, fw2, fb2, fw3, fb3, fw_out, freq, emb, window, dr, di):
    L = emb.shape[0]
    NF = dr.shape[0]
    C = HY_CHANNELS
    w1 = jnp.pad(fw1.astype(F32), ((0, HY_EMB_PAD - HY_EMB), (0, 0)))
    wo = fw_out.astype(F32).reshape(HY_HID, HY_ORDER, 2, C).transpose(1, 2, 0, 3)
    vec = lambda v: v.astype(F32).reshape(1, HY_HID)
    full = lambda a: pl.BlockSpec(a.shape, lambda o, c: (0,) * a.ndim)
    args = (emb, w1, vec(fb1), fw2.astype(F32), vec(fb2), fw3.astype(F32), vec(fb3))
    once = pl.Buffered(1)
    out = pl.BlockSpec((1, NF, HY_FILT_CB), lambda o, c: (o, 0, c))
    return pl.pallas_call(
        _filt_kernel, grid=(HY_ORDER, C // HY_FILT_CB),
        in_specs=[full(a) for a in args] + [
            pl.BlockSpec((1, 2, HY_HID, HY_FILT_CB), lambda o, c: (o, 0, 0, c)),
            pl.BlockSpec((1, HY_HID), lambda o, c: (0, 0)),
            pl.BlockSpec((L, HY_FILT_CB), lambda o, c: (0, c)),
            pl.BlockSpec((NF, L), lambda o, c: (0, 0), pipeline_mode=once),
            pl.BlockSpec((NF, L), lambda o, c: (0, 0), pipeline_mode=once)],
        out_specs=[out, out],
        out_shape=[jax.ShapeDtypeStruct((HY_ORDER, NF, C), F32)] * 2,
        compiler_params=pltpu.CompilerParams(dimension_semantics=("arbitrary", "arbitrary"),
                                             vmem_limit_bytes=VMEM_LIMIT_HYENA),
        name="hyena_filter")(*args, wo, vec(freq), window, dr, di)


def _conv_kernel(v_ref, x1_ref, x2_ref, cw_ref, cb_ref, skip_ref, hr_ref, hi_ref, dr_ref, di_ref, wk_ref, o_ref):
    S = v_ref.shape[2]
    nsl = v_ref.shape[0]
    row = lax.broadcasted_iota(jnp.int32, (S, LANES), 0)

    def short(u_ref, g):
        cols = []
        for s in range(nsl):
            u = u_ref[s, 0]
            w = cw_ref[g, s]
            up = jnp.where(row == 0, 0.0, pltpu.roll(u, 1, 0))
            un = jnp.where(row == S - 1, 0.0, pltpu.roll(u, S - 1, 0))
            cols.append(up * w[0:1] + u * w[1:2] + un * w[2:3] + cb_ref[g, s])
        return jnp.concatenate(cols, axis=-1)

    d = lambda x, y: jnp.dot(x, y, preferred_element_type=F32)
    wk = wk_ref[...]

    def long_conv(z, o):
        zb = z.astype(BF16)
        zr = d(dr_ref[...], zb)
        zi = d(di_ref[...], zb)
        hr = hr_ref[o]
        hi = hi_ref[o]
        yr = ((zr * hr - zi * hi) * wk).T.astype(BF16)
        yi = ((zr * hi + zi * hr) * wk).T.astype(BF16)
        y = d(yr, dr_ref[...]) + d(yi, di_ref[...])
        return y.T + z * skip_ref[o]

    v = short(v_ref, 0)
    z = short(x1_ref, 1) * long_conv(v, 0)
    out = short(x2_ref, 2) * long_conv(z, 1)
    for s in range(nsl):
        o_ref[s, 0] = out[:, s * LANES:(s + 1) * LANES]


def _hyena_conv(slabs, conv_w, conv_b, skip, hr, hi, dr, di, wk, B, S):
    C = HY_CHANNELS
    NF = dr.shape[0]
    nsl = HY_CB // LANES
    nc = C // HY_CB
    x = slabs.reshape(slabs.shape[0], B, S, LANES)
    cw = conv_w.astype(F32).reshape(3, 3, C // LANES, LANES).transpose(1, 2, 0, 3)
    cb = conv_b.astype(F32).reshape(3, C // LANES, 1, LANES)
    sk = skip.astype(F32).reshape(HY_ORDER, 1, C)
    spec = lambda g: pl.BlockSpec((nsl, 1, S, LANES), lambda c, b, g=g: (g * nc + c, b, 0, 0))
    once = pl.Buffered(1)
    return pl.pallas_call(
        _conv_kernel, grid=(nc, B),
        in_specs=[spec(0), spec(1), spec(2),
                  pl.BlockSpec((3, nsl, 3, LANES), lambda c, b: (0, c, 0, 0)),
                  pl.BlockSpec((3, nsl, 1, LANES), lambda c, b: (0, c, 0, 0)),
                  pl.BlockSpec((HY_ORDER, 1, HY_CB), lambda c, b: (0, 0, c)),
                  pl.BlockSpec((HY_ORDER, NF, HY_CB), lambda c, b: (0, 0, c), pipeline_mode=once),
                  pl.BlockSpec((HY_ORDER, NF, HY_CB), lambda c, b: (0, 0, c), pipeline_mode=once),
                  pl.BlockSpec((NF, S), lambda c, b: (0, 0), pipeline_mode=once),
                  pl.BlockSpec((NF, S), lambda c, b: (0, 0), pipeline_mode=once),
                  pl.BlockSpec((NF, 1), lambda c, b: (0, 0))],
        out_specs=pl.BlockSpec((nsl, 1, S, LANES), lambda c, b: (c, b, 0, 0)),
        out_shape=jax.ShapeDtypeStruct((C // LANES, B, S, LANES), F32),
        compiler_params=pltpu.CompilerParams(dimension_semantics=("arbitrary", "arbitrary"),
                                             vmem_limit_bytes=VMEM_LIMIT_HYENA),
        name="hyena_conv")(x, x, x, cw, cb, sk, hr, hi, dr, di, wk)


def _ret_kernel(lg_ref, q_ref, k_ref, v_ref, g_ref, cos_ref, sin_ref, gn_ref, o_ref, qr_ref, kr_ref, vb_ref):
    hp = pl.program_id(0)
    S = q_ref.shape[1]
    cos = cos_ref[...]
    sin = sin_ref[...]
    qr_ref[...] = _rope2(q_ref[0], cos, sin).astype(BF16)
    kr_ref[...] = (_rope2(k_ref[0], cos, sin) * HEAD_DIM ** -0.5).astype(BF16)
    vb_ref[...] = v_ref[0].astype(BF16)
    first = lax.broadcasted_iota(jnp.int32, (RET_BQ, LANES), 1) < HEAD_DIM
    m_idx = lax.broadcasted_iota(jnp.int32, (RET_BQ, S), 1)
    n_loc = lax.broadcasted_iota(jnp.int32, (RET_BQ, S), 0)

    def body(r, c):
        row0 = pl.multiple_of(r * RET_BQ, RET_BQ)
        diff = (n_loc + row0 - m_idx).astype(F32)
        q2 = qr_ref[pl.ds(row0, RET_BQ), :]
        ys = []
        for hh in range(2):
            lgf = lg_ref[0, 2 * hp + hh]
            lgb = lg_ref[1, 2 * hp + hh]
            dec = jnp.exp(diff * jnp.where(diff >= 0.0, lgf, -lgb))
            qm = jnp.where(first if hh == 0 else ~first, q2, jnp.zeros_like(q2))
            s = lax.dot_general(qm, kr_ref[...], (((1,), (1,)), ((), ())), preferred_element_type=F32)
            ys.append(jnp.dot((s * dec).astype(BF16), vb_ref[...], preferred_element_type=F32))
        y = jnp.where(first, ys[0], ys[1])
        y = y * lax.rsqrt(_pair_sums(y * y) * (1.0 / HEAD_DIM) + EPS) * gn_ref[0]
        g = g_ref[0, pl.ds(row0, RET_BQ), :]
        o_ref[0, pl.ds(row0, RET_BQ), :] = g * jax.nn.sigmoid(g) * y
        return c

    lax.fori_loop(0, S // RET_BQ, body, 0)


def _retention(slabs, first_slab, log_decay_param, gn_gain, cos, sin, B, S):
    HP = HEAD_PAIRS
    x = slabs.reshape(slabs.shape[0], B, S, LANES)
    lg = -jnp.exp(log_decay_param.astype(F32))
    gn = gn_gain.astype(F32).reshape(HP, 1, LANES)
    spec = lambda g: pl.BlockSpec((None, 1, S, LANES), lambda h, b, g=g: (first_slab + g * HP + h, b, 0, 0))
    tab = pl.BlockSpec((S, LANES), lambda h, b: (0, 0))
    return pl.pallas_call(
        _ret_kernel, grid=(HP, B),
        in_specs=[pl.BlockSpec(memory_space=pltpu.SMEM), spec(0), spec(1), spec(2), spec(3), tab, tab,
                  pl.BlockSpec((None, 1, LANES), lambda h, b: (h, 0, 0))],
        out_specs=pl.BlockSpec((None, 1, S, LANES), lambda h, b: (h, b, 0, 0)),
        out_shape=jax.ShapeDtypeStruct((HP, B, S, LANES), F32),
        scratch_shapes=[pltpu.VMEM((S, LANES), BF16)] * 3,
        compiler_params=pltpu.CompilerParams(dimension_semantics=("arbitrary", "arbitrary"),
                                             vmem_limit_bytes=VMEM_LIMIT),
        name="retention")(lg, x, x, x, x, cos, sin, gn)


def _na_kernel(q_ref, k_ref, v_ref, gq_ref, gk_ref, bias_ref, o_ref, qn_ref, kn_ref, *, rows):
    scale = HEAD_DIM ** -0.5
    qn_ref[...] = (_head_rms2(q_ref[0], gq_ref[...]) * scale).astype(BF16)
    kn_ref[...] = _head_rms2(k_ref[0], gk_ref[...]).astype(BF16)
    lane = lax.broadcasted_iota(jnp.int32, (GRID_W, LANES), 1)
    first = lane < HEAD_DIM
    half_rows = NA_ROWS // 2

    def body(r, c):
        r0 = jnp.clip(r - half_rows, 0, rows - NA_ROWS)
        pat = jnp.where(r < half_rows, r,
                        jnp.where(r > rows - half_rows, r - (rows - NA_ROWS), half_rows))
        q2 = qn_ref[pl.ds(pl.multiple_of(r * GRID_W, GRID_W), GRID_W), :]
        kstart = pl.multiple_of(r0 * GRID_W, GRID_W)
        k2 = kn_ref[pl.ds(kstart, NA_ROWS * GRID_W), :]
        v2 = v_ref[0, pl.ds(kstart, NA_ROWS * GRID_W), :].astype(BF16)
        outs = []
        for hh in range(2):
            qm = jnp.where(first if hh == 0 else ~first, q2, jnp.zeros_like(q2))
            s = lax.dot_general(qm, k2, (((1,), (1,)), ((), ())), preferred_element_type=F32)
            s = s + bias_ref[hh, pat]
            m = jnp.max(s, axis=-1, keepdims=True)
            e = jnp.exp(s - m)
            l = jnp.sum(e, axis=-1, keepdims=True)
            outs.append(jnp.dot(e.astype(BF16), v2, preferred_element_type=F32) / l)
        o_ref[0, pl.ds(pl.multiple_of(r * GRID_W, GRID_W), GRID_W), :] = jnp.where(first, outs[0], outs[1])
        return c

    lax.fori_loop(0, rows, body, 0, unroll=2)


def _na_bias_table(rpb, rows):
    kh = NA_ROWS
    c = np.arange(GRID_W)
    c0 = np.clip(c - NA_COLS // 2, 0, GRID_W - NA_COLS)
    col_valid = (c[None, :] >= c0[:, None]) & (c[None, :] < c0[:, None] + NA_COLS)
    col_off = np.clip(c[None, :] - c[:, None] + NA_COLS - 1, 0, 2 * NA_COLS - 2)
    pat_rows = list(range(kh // 2)) + [kh // 2] + list(range(rows - kh // 2 + 1, rows))
    row_sel = np.zeros((len(pat_rows), kh, 2 * NA_ROWS - 1), np.float32)
    for p, r in enumerate(pat_rows):
        r0 = min(max(r - kh // 2, 0), rows - kh)
        row_sel[p, np.arange(kh), r0 + np.arange(kh) - r + NA_ROWS - 1] = 1.0
    col_sel = (col_off[None] == np.arange(2 * NA_COLS - 1)[:, None, None]).astype(np.float32)
    b = jnp.einsum('pir,hrc,cqk->hpqik', row_sel, rpb.astype(F32), col_sel, precision=lax.Precision.HIGHEST)
    b = jnp.where(col_valid[None, None, :, None, :], b, NEG)
    return b.reshape(rpb.shape[0], len(pat_rows), GRID_W, kh * GRID_W)


def _neighbourhood_attention(slabs, gq, gk, rpb, B, S):
    HP = HEAD_PAIRS
    rows = S // GRID_W
    assert rows >= NA_ROWS
    n_pat = NA_ROWS
    bias = _na_bias_table(rpb, rows).reshape(HP, 2, n_pat, GRID_W, NA_ROWS * GRID_W)
    g2q = jnp.tile(gq.astype(F32), 2).reshape(1, LANES)
    g2k = jnp.tile(gk.astype(F32), 2).reshape(1, LANES)
    x = slabs.reshape(slabs.shape[0], B, S, LANES)
    spec = lambda g: pl.BlockSpec((None, 1, S, LANES), lambda h, b, g=g: (g * HP + h, b, 0, 0))
    vec = pl.BlockSpec((1, LANES), lambda h, b: (0, 0))
    return pl.pallas_call(
        functools.partial(_na_kernel, rows=rows), grid=(HP, B),
        in_specs=[spec(0), spec(1), spec(2), vec, vec,
                  pl.BlockSpec((None, 2, n_pat, GRID_W, NA_ROWS * GRID_W), lambda h, b: (h, 0, 0, 0, 0))],
        out_specs=pl.BlockSpec((None, 1, S, LANES), lambda h, b: (h, b, 0, 0)),
        out_shape=jax.ShapeDtypeStruct((HP, B, S, LANES), F32),
        scratch_shapes=[pltpu.VMEM((S, LANES), BF16), pltpu.VMEM((S, LANES), BF16)],
        compiler_params=pltpu.CompilerParams(dimension_semantics=("arbitrary", "arbitrary"),
                                             vmem_limit_bytes=VMEM_LIMIT),
        name="nbr_attention")(x, x, x, g2q, g2k, bias)


def _dil_kernel(q_ref, k_ref, v_ref, gq_ref, gk_ref, cos_ref, sin_ref, o_ref,
                qn_ref, kn_ref, qd_ref, kd_ref, vd_ref, od_ref, ld_ref, os_ref, ls_ref):
    S = q_ref.shape[1]
    BLK = DIL_BLK
    BQ = DIL_BQ
    NK = BQ + 2 * BLK
    cos = cos_ref[...]
    sin = sin_ref[...]
    qn_ref[...] = _rope2(_head_rms2(q_ref[0], gq_ref[...]), cos, sin) * HEAD_DIM ** -0.5
    kn_ref[...] = _rope2(_head_rms2(k_ref[0], gk_ref[...]), cos, sin)
    first = lax.broadcasted_iota(jnp.int32, (BQ, LANES), 1) < HEAD_DIM
    qi = lax.broadcasted_iota(jnp.int32, (BQ, NK), 0)
    kc = lax.broadcasted_iota(jnp.int32, (BQ, NK), 1) - BLK
    band = jnp.abs(kc - qi) <= BLK
    for bi, (window, dil) in enumerate(DIL_BRANCHES):
        assert window // (2 * dil) == BLK
        Ls = S // dil
        shift = Ls.bit_length() - 1
        assert 1 << shift == Ls and Ls >= 2 * BLK
        for r in range(dil):
            sl = pl.ds(r, Ls, stride=dil) if dil > 1 else pl.ds(0, Ls)
            qd_ref[pl.ds(r * Ls, Ls), :] = qn_ref[sl, :].astype(BF16)
            kd_ref[pl.ds(r * Ls, Ls), :] = kn_ref[sl, :].astype(BF16)
            vd_ref[pl.ds(r * Ls, Ls), :] = v_ref.at[0][sl, :].astype(BF16)

        def body(fb, c):
            row0 = pl.multiple_of(fb * BQ, BQ)
            prev0 = pl.multiple_of(jnp.maximum(row0 - BLK, 0), BLK)
            next0 = pl.multiple_of(jnp.minimum(row0 + BQ, S - BLK), BLK)
            kk = jnp.concatenate([kd_ref[pl.ds(prev0, BLK), :], kd_ref[pl.ds(row0, BQ), :],
                                  kd_ref[pl.ds(next0, BLK), :]], axis=0)
            vv = jnp.concatenate([vd_ref[pl.ds(prev0, BLK), :], vd_ref[pl.ds(row0, BQ), :],
                                  vd_ref[pl.ds(next0, BLK), :]], axis=0)
            valid = band & (((row0 + kc) >> shift) == ((row0 + qi) >> shift))
            q2 = qd_ref[pl.ds(row0, BQ), :]
            outs, lses = [], []
            for hh in range(2):
                qm = jnp.where(first if hh == 0 else ~first, q2, jnp.zeros_like(q2))
                s = lax.dot_general(qm, kk, (((1,), (1,)), ((), ())), preferred_element_type=F32)
                s = jnp.where(valid, s, NEG)
                m = jnp.max(s, axis=-1, keepdims=True)
                p = jnp.exp(s - m)
                den = jnp.sum(p, axis=-1, keepdims=True)
                outs.append(jnp.dot(p.astype(BF16), vv, preferred_element_type=F32) / den)
                lses.append(m + jnp.log(den))
            od_ref[pl.ds(row0, BQ), :] = jnp.where(first, outs[0], outs[1])
            ld_ref[pl.ds(row0, BQ), :] = jnp.where(first, lses[0], lses[1])
            return c

        lax.fori_loop(0, S // BQ, body, 0, unroll=2)
        for r in range(dil):
            sl = pl.ds(r, Ls, stride=dil) if dil > 1 else pl.ds(0, Ls)
            os_ref.at[bi][sl, :] = od_ref[pl.ds(r * Ls, Ls), :]
            ls_ref.at[bi][sl, :] = ld_ref[pl.ds(r * Ls, Ls), :]
    l0, l1, l2 = ls_ref[0], ls_ref[1], ls_ref[2]
    mx = jnp.maximum(jnp.maximum(l0, l1), l2)
    w0, w1, w2 = jnp.exp(l0 - mx), jnp.exp(l1 - mx), jnp.exp(l2 - mx)
    o_ref[0] = (w0 * os_ref[0] + w1 * os_ref[1] + w2 * os_ref[2]) / (w0 + w1 + w2)


def _dilated_attention(slabs, first_slab, gq, gk, cos, sin, B, S):
    HP = HEAD_PAIRS
    x = slabs.reshape(slabs.shape[0], B, S, LANES)
    g2q = jnp.tile(gq.astype(F32), 2).reshape(1, LANES)
    g2k = jnp.tile(gk.astype(F32), 2).reshape(1, LANES)
    spec = lambda g: pl.BlockSpec((None, 1, S, LANES), lambda h, b, g=g: (first_slab + g * HP + h, b, 0, 0))
    tab = pl.BlockSpec((S, LANES), lambda h, b: (0, 0))
    vec = pl.BlockSpec((1, LANES), lambda h, b: (0, 0))
    return pl.pallas_call(
        _dil_kernel, grid=(HP, B),
        in_specs=[spec(0), spec(1), spec(2), vec, vec, tab, tab],
        out_specs=pl.BlockSpec((None, 1, S, LANES), lambda h, b: (h, b, 0, 0)),
        out_shape=jax.ShapeDtypeStruct((HP, B, S, LANES), F32),
        scratch_shapes=[pltpu.VMEM((S, LANES), F32), pltpu.VMEM((S, LANES), F32),
                        pltpu.VMEM((S, LANES), BF16), pltpu.VMEM((S, LANES), BF16),
                        pltpu.VMEM((S, LANES), BF16),
                        pltpu.VMEM((S, LANES), F32), pltpu.VMEM((S, LANES), F32),
                        pltpu.VMEM((len(DIL_BRANCHES), S, LANES), F32),
                        pltpu.VMEM((len(DIL_BRANCHES), S, LANES), F32)],
        compiler_params=pltpu.CompilerParams(dimension_semantics=("arbitrary", "arbitrary"),
                                             vmem_limit_bytes=VMEM_LIMIT),
        name="dilated_attention")(x, x, x, g2q, g2k, cos, sin)


def _stair_tables():
    a_idx = [0] * 16 + [a for a in range(1, 8) for _ in range(8)] + list(range(8, 16))
    b_idx = list(range(16)) + [b for _ in range(1, 8) for b in range(8)] + [0] * 8
    a_idx = np.array(a_idx)
    b_idx = np.array(b_idx)
    valid = (a_idx + 1) * (b_idx + 1) <= PEER_TOPK
    return a_idx, b_idx, valid


_ST_A, _ST_B, _ST_VALID = _stair_tables()
N_CAND = len(_ST_A)


def _top16_rows(sc, iota):
    vals, idxs = [], []
    big = float(sc.shape[0])
    for r in range(PEER_TOPK):
        m = jnp.max(sc, axis=0, keepdims=True)
        idx = jnp.min(jnp.where(sc == m, iota, big), axis=0, keepdims=True)
        vals.append(m)
        idxs.append(idx)
        if r + 1 < PEER_TOPK:
            sc = jnp.where(iota == idx, NEG_INF, sc)
    return jnp.concatenate(vals, axis=0), jnp.concatenate(idxs, axis=0)


def _route_kernel(xn_ref, wqT_ref, keys_ref, flat_ref, e_ref, g_ref, qT_ref, eT_ref, gT_ref):
    Tb = xn_ref.shape[0]
    xb = xn_ref[...].astype(BF16)
    qT_ref[...] = lax.dot_general(wqT_ref[...], xb, (((1,), (1,)), ((), ())), preferred_element_type=F32)
    iota = lax.broadcasted_iota(jnp.int32, (PEER_KEYS, Tb), 0).astype(F32)
    flat = jnp.broadcast_to(flat_ref[...], (N_CAND, Tb))
    for h in range(PEER_HEADS):
        s, ix = [], []
        for p in range(2):
            r0 = (h * 2 + p) * (PEER_DK // 2)
            sc = jnp.dot(keys_ref[h * 2 + p], qT_ref[pl.ds(r0, PEER_DK // 2), :].astype(BF16),
                         preferred_element_type=F32)
            sp, ip = _top16_rows(sc, iota)
            s.append(sp)
            ix.append(ip)
        s0, s1 = s
        i0, i1 = ix[0] * float(PEER_KEYS), ix[1]
        parts_s = [s0[0:1] + s1]
        parts_e = [i0[0:1] + i1]
        for a in range(1, 8):
            parts_s.append(s0[a:a + 1] + s1[0:8])
            parts_e.append(i0[a:a + 1] + i1[0:8])
        parts_s.append(s0[8:16] + s1[0:1])
        parts_e.append(i0[8:16] + i1[0:1])
        cand = jnp.where(flat < 256.0, jnp.concatenate(parts_s, axis=0), NEG_INF)
        eid = jnp.concatenate(parts_e, axis=0)
        gs, es = [], []
        for r in range(PEER_TOPK):
            m = jnp.max(cand, axis=0, keepdims=True)
            fsel = jnp.min(jnp.where(cand == m, flat, float(1 << 20)), axis=0, keepdims=True)
            hit = flat == fsel
            es.append(jnp.max(jnp.where(hit, eid, -1.0), axis=0, keepdims=True))
            gs.append(m)
            if r + 1 < PEER_TOPK:
                cand = jnp.where(hit, NEG_INF, cand)
        g = jnp.concatenate(gs, axis=0)
        ex = jnp.exp(g - g[0:1])
        eT_ref[pl.ds(h * PEER_TOPK, PEER_TOPK), :] = jnp.concatenate(es, axis=0).astype(jnp.int32)
        gT_ref[pl.ds(h * PEER_TOPK, PEER_TOPK), :] = ex / jnp.sum(ex, axis=0, keepdims=True)
    e_ref[...] = eT_ref[...].T
    g_ref[...] = gT_ref[...].T


def _peer_route(xn, w_q, sub_keys):
    T = xn.shape[0]
    Tb = ROUTE_TOK
    wqT = w_q.T.astype(BF16)
    keys = sub_keys.reshape(PEER_HEADS * 2, PEER_KEYS, PEER_DK // 2).astype(BF16)
    flat = np.where(_ST_VALID, _ST_A * 16 + _ST_B, 1 << 16).astype(np.float32).reshape(N_CAND, 1)
    return pl.pallas_call(
        _route_kernel, grid=(T // Tb,),
        in_specs=[pl.BlockSpec((Tb, D_MODEL), lambda i: (i, 0)),
                  pl.BlockSpec((D_MODEL, D_MODEL), lambda i: (0, 0)),
                  pl.BlockSpec((PEER_HEADS * 2, PEER_KEYS, PEER_DK // 2), lambda i: (0, 0, 0)),
                  pl.BlockSpec((N_CAND, 1), lambda i: (0, 0))],
        out_specs=[pl.BlockSpec((Tb, PEER_SEL), lambda i: (i, 0)),
                   pl.BlockSpec((Tb, PEER_SEL), lambda i: (i, 0))],
        out_shape=[jax.ShapeDtypeStruct((T, PEER_SEL), jnp.int32),
                   jax.ShapeDtypeStruct((T, PEER_SEL), F32)],
        scratch_shapes=[pltpu.VMEM((D_MODEL, Tb), F32), pltpu.VMEM((PEER_SEL, Tb), jnp.int32),
                        pltpu.VMEM((PEER_SEL, Tb), F32)],
        compiler_params=pltpu.CompilerParams(dimension_semantics=("arbitrary",),
                                             vmem_limit_bytes=VMEM_LIMIT),
        name="peer_route")(xn, wqT, keys, jnp.asarray(flat))


def _peer_eval_kernel(idx_cur, idx_nxt, x_ref, r_ref, g_ref, uv_hbm, o_ref, buf, sem):
    i = pl.program_id(0)
    n = pl.num_programs(0)
    G = PEER_TOK
    NS = PEER_SLOTS
    assert PEER_AHEAD == NS - 1

    def issue_rows(idx_ref, row, slot, g, k0, k1):
        for k in range(k0, k1):
            e = idx_ref[row, k]
            pltpu.make_async_copy(uv_hbm.at[e], buf.at[slot, g, k // SUBLANES, :, k % SUBLANES, :],
                                  sem.at[slot]).start(priority=k % DMA_THREADS)

    def wait_slot(slot):
        pltpu.make_async_copy(buf.at[slot], buf.at[slot], sem.at[slot]).wait()

    def compute_tok(slot, g, row, gates_t, issue):
        per = PEER_SEL // SUBLANES
        xg = x_ref[row]
        acc = buf[slot, g, :, 0].reshape(PEER_SEL, LANES) * xg[0:1, :]
        for c in range(1, SUBLANES):
            acc = acc + buf[slot, g, :, c].reshape(PEER_SEL, LANES) * xg[c:c + 1, :]
        h = jnp.sum(acc, axis=-1, keepdims=True)
        a = jax.nn.gelu(h) * gates_t[:, row:row + 1]
        ab = jnp.broadcast_to(a, (PEER_SEL, LANES))
        rows = []
        for c in range(SUBLANES):
            issue(c * per, (c + 1) * per)
            rows.append(jnp.sum(ab * buf[slot, g, :, SUBLANES + c].reshape(PEER_SEL, LANES),
                                axis=0, keepdims=True))
        o_ref[row] = r_ref[row] + jnp.concatenate(rows, axis=0)

    @pl.when(i == 0)
    def _():
        for j in range(PEER_AHEAD):
            for g in range(G):
                issue_rows(idx_cur, j * G + g, j, g, 0, PEER_SEL)

    gates_t = g_ref[...].T
    for j in range(NS):
        wait_slot(j)
        ahead = j + PEER_AHEAD
        for g in range(G):
            if ahead < NS:
                src, row, slot = idx_cur, ahead * G + g, ahead
            else:
                src, row, slot = idx_nxt, (ahead - NS) * G + g, ahead - NS
            compute_tok(j, g, j * G + g, gates_t,
                        functools.partial(issue_rows, src, row, slot, g))

    @pl.when(i == n - 1)
    def _():
        for j in range(PEER_AHEAD):
            wait_slot(j)


def _peer_eval(xt, res, experts, gates, uv):
    T = xt.shape[0]
    G = PEER_TOK
    NT = PEER_SLOTS * G
    n = T // NT
    x3 = xt.reshape(T, SUBLANES, LANES)
    r3 = res.reshape(T, SUBLANES, LANES)
    tok = pl.BlockSpec((NT, SUBLANES, LANES), lambda i: (i, 0, 0))
    out = pl.pallas_call(
        _peer_eval_kernel,
        grid=(n,),
        in_specs=[
            pl.BlockSpec((NT, PEER_SEL), lambda i: (i, 0), memory_space=pltpu.SMEM),
            pl.BlockSpec((NT, PEER_SEL), lambda i: (jnp.minimum(i + 1, n - 1), 0),
                         memory_space=pltpu.SMEM),
            tok,
            tok,
            pl.BlockSpec((NT, PEER_SEL), lambda i: (i, 0)),
            pl.BlockSpec(memory_space=pl.ANY),
        ],
        out_specs=tok,
        out_shape=jax.ShapeDtypeStruct((T, SUBLANES, LANES), F32),
        scratch_shapes=[
            pltpu.VMEM((PEER_SLOTS, G, 2 * SUBLANES, 2 * SUBLANES, SUBLANES, LANES), F32),
            pltpu.SemaphoreType.DMA((PEER_SLOTS,)),
        ],
        compiler_params=pltpu.CompilerParams(
            dimension_semantics=("arbitrary",),
            vmem_limit_bytes=VMEM_LIMIT),
        name="peer_eval",
    )(experts, experts, x3, r3, gates, uv)
    return out.reshape(T, D_MODEL)


def _peer(xn, res, w_q, sub_keys, u, v):
    E = u.shape[0]
    experts, gates = _peer_route(xn, w_q, sub_keys)
    uv = jnp.concatenate([u.astype(F32).reshape(E, SUBLANES, LANES),
                          v.astype(F32).reshape(E, SUBLANES, LANES)], axis=1)
    return _peer_eval(xn, res, experts, gates, uv)


def kernel(x, mix_norm_gain, ffn_norm_gain, ev_w_in, ev_w_out, hy_conv_w, hy_conv_b, hy_fw1, hy_fb1, hy_fw2, hy_fb2, hy_fw3, hy_fb3, hy_fw_out, hy_freq, hy_skip, ret_log_decay, ret_gn_gain, od_w_in, od_w_out, na_qk_gain, na_rpb, dil_qk_gain, peer_w_q, peer_sub_keys, peer_u, peer_v):
    B, S, D = x.shape
    T = B * S
    depth = mix_norm_gain.shape[0]
    cos, sin = _rope_tables(S)
    emb, window = _filter_consts(S)
    dr, di, wk = _dft_tables(S)
    xt = x.reshape(T, D).astype(F32)
    for layer in range(depth):
        if layer % 2 == 0:
            e = layer // 2
            slabs = _norm_in_proj(xt, mix_norm_gain[layer], ev_w_in[e])
            hr, hi = _hyena_filter_spectrum(hy_fw1[e], hy_fb1[e], hy_fw2[e], hy_fb2[e], hy_fw3[e], hy_fb3[e],
                                            hy_fw_out[e], hy_freq[e], emb, window, dr, di)
            a_out = _hyena_conv(slabs, hy_conv_w[e], hy_conv_b[e], hy_skip[e], hr, hi, dr, di, wk, B, S)
            b_out = _retention(slabs, 3 * HEAD_PAIRS, ret_log_decay[e], ret_gn_gain[e], cos, sin, B, S)
            w_out = ev_w_out[e]
        else:
            o = layer // 2
            slabs = _norm_in_proj(xt, mix_norm_gain[layer], od_w_in[o])
            a_out = _neighbourhood_attention(slabs, na_qk_gain[o, 0], na_qk_gain[o, 1], na_rpb[o], B, S)
            b_out = _dilated_attention(slabs, 3 * HEAD_PAIRS, dil_qk_gain[o, 0], dil_qk_gain[o, 1],
                                       cos, sin, B, S)
            w_out = od_w_out[o]
        xt, xn = _out_proj_residual(a_out.reshape(HEAD_PAIRS, T, LANES), b_out.reshape(HEAD_PAIRS, T, LANES),
                                    w_out, xt, ffn_norm_gain[layer])
        xt = _peer(xn, xt, peer_w_q[layer], peer_sub_keys[layer], peer_u[layer], peer_v[layer])
    return xt.reshape(B, S, D).astype(x.dtype)
```
